```python
import jax, jax.numpy as jnp
from jax import lax
import numpy as np

D_MODEL = 1024
BATCH = 4
SEQ = 4096
DEPTH = 2
DEC_BATCH = 128
DEC_SEQ = 4
PAST_LEN = 2048
PAGE_SIZE = 128

HEAD_DIM = 64
NSA_HEADS = 8
NSA_KV_HEADS = 2
NSA_GROUP = NSA_HEADS // NSA_KV_HEADS
CMP_BLOCK = 32
SEL_BLOCK = 64
TOP_N = 16
WINDOW = 512
SEL_QUERY_BLOCK = 64
FORCED_SCORE = 1e4
MLSTM_HEADS = 4
MLSTM_DIM = 128
MLSTM_CHUNK = 64
SB_HEADS = D_MODEL // HEAD_DIM
QUERY_BLOCK = 128
D_FF = 2816
N_EXPERTS = 8
TOP_K = 2
D_FF_EXPERT = 3584
EPS = 1e-6
NEG_BIG = -1e30
TINY = 1e-30

IN0_SIZES = (NSA_HEADS * HEAD_DIM,) + (NSA_KV_HEADS * HEAD_DIM,) * 6 + (3 * NSA_HEADS,) + (MLSTM_HEADS * MLSTM_DIM,) * 4 + (MLSTM_HEADS, MLSTM_HEADS)
IN0_COLS = sum(IN0_SIZES)
MIX0_WIDTH = NSA_HEADS * HEAD_DIM + MLSTM_HEADS * MLSTM_DIM

kernel_name = 'nsa_mlstm_stickbreak_moe_decode_step'


def rmsnorm(x, g):
    xf = x.astype(jnp.float32)
    y = xf * lax.rsqrt(jnp.mean(xf * xf, axis=-1, keepdims=True) + EPS)
    return (y * g.astype(jnp.float32)).astype(x.dtype)


def head_rmsnorm(h, g, dtype):
    y = h * lax.rsqrt(jnp.mean(h * h, axis=-1, keepdims=True) + EPS)
    return (y * g.astype(jnp.float32)).astype(dtype)


def masked_softmax(s, valid, axes):
    s = jnp.where(valid, s, NEG_BIG)
    m = jnp.max(s, axis=axes, keepdims=True)
    e = jnp.where(valid, jnp.exp(s - m), 0.0)
    return e / jnp.maximum(jnp.sum(e, axis=axes, keepdims=True), TINY)


def split_cols(h, sizes):
    offs = [int(o) for o in np.cumsum(sizes)[:-1]]
    return jnp.split(h, offs, axis=-1)


def swiglu(x, wg, wu, wd):
    return (jax.nn.silu(x @ wg) * (x @ wu)) @ wd


def moe_swiglu(x, w_router, wg, wu, wd):
    probs = jax.nn.softmax((x @ w_router).astype(jnp.float32), axis=-1)
    top_w, top_i = lax.top_k(probs, TOP_K)
    top_w = top_w / jnp.sum(top_w, axis=-1, keepdims=True)
    combine = jnp.sum(jax.nn.one_hot(top_i, N_EXPERTS, dtype=jnp.float32) * top_w[..., None], axis=-2)
    y = jnp.zeros_like(x)
    for e in range(N_EXPERTS):
        y = y + combine[..., e:e + 1].astype(x.dtype) * swiglu(x, wg[e], wu[e], wd[e])
    return y


def nsa_compress(rows, pos_emb, w):
    B, Tk, G, dh = rows.shape
    blk = rows.reshape(B, Tk // CMP_BLOCK, CMP_BLOCK, G, dh) + pos_emb[None, None, :, None, :]
    blk = blk.transpose(0, 1, 3, 2, 4).reshape(B, Tk // CMP_BLOCK, G, CMP_BLOCK * dh)
    return blk @ w


def nsa_compressed_selected(q, rows, q_pos, cmp_pos, cmp_w):
    B, Tq, H, dh = q.shape
    G, R = NSA_KV_HEADS, NSA_GROUP
    pad = (-rows.shape[1]) % SEL_BLOCK
    rows = jnp.pad(rows, ((0, 0), (0, pad), (0, 0), (0, 0), (0, 0)))
    Tp = rows.shape[1]
    scale = dh ** -0.5
    qg = q.reshape(B, Tq, G, R, dh)
    k_cmp = nsa_compress(rows[:, :, 0], cmp_pos[0], cmp_w[0])
    v_cmp = nsa_compress(rows[:, :, 1], cmp_pos[1], cmp_w[1])
    n_cmp = Tp // CMP_BLOCK
    s = jnp.einsum('btgrd,bngd->bgrtn', qg, k_cmp).astype(jnp.float32) * scale
    cmp_end = (jnp.arange(n_cmp) + 1) * CMP_BLOCK - 1
    p = masked_softmax(s, cmp_end[None, :] <= q_pos[:, None], -1)
    o_cmp = jnp.einsum('bgrtn,bngd->btgrd', p.astype(v_cmp.dtype), v_cmp).reshape(B, Tq, H, dh)
    n_sel = Tp // SEL_BLOCK
    imp = p.sum(axis=2).reshape(B, G, Tq, n_sel, SEL_BLOCK // CMP_BLOCK).sum(axis=-1)
    blk = jnp.arange(n_sel)
    forced = (blk[None, :] == 0) | (blk[None, :] == (q_pos // SEL_BLOCK)[:, None])
    avail = blk[None, :] * SEL_BLOCK <= q_pos[:, None]
    imp = jnp.where(avail, jnp.where(forced, FORCED_SCORE, imp), -1.0)
    _, idx = lax.top_k(imp, min(TOP_N, n_sel))
    n_top = idx.shape[-1]
    ks = rows[:, :, 2].reshape(B, n_sel, SEL_BLOCK, G, dh).transpose(0, 3, 1, 2, 4)
    vs = rows[:, :, 3].reshape(B, n_sel, SEL_BLOCK, G, dh).transpose(0, 3, 1, 2, 4)
    qb = SEL_QUERY_BLOCK if Tq % SEL_QUERY_BLOCK == 0 else Tq
    nqb = Tq // qb
    q_blocks = qg.reshape(B, nqb, qb, G, R, dh).swapaxes(0, 1)
    idx_blocks = idx.reshape(B, G, nqb, qb, n_top).transpose(2, 0, 1, 3, 4)
    pos_blocks = q_pos.reshape(nqb, qb)
    b_ix = jnp.arange(B)[:, None, None, None]
    g_ix = jnp.arange(G)[None, :, None, None]

    def sel_block(args):
        qc, ic, pc = args
        kg = ks[b_ix, g_ix, ic]
        vg = vs[b_ix, g_ix, ic]
        sc = jnp.einsum('bqgrd,bgqnkd->bgrqnk', qc, kg).astype(jnp.float32) * scale
        kpos = ic[..., None] * SEL_BLOCK + jnp.arange(SEL_BLOCK)
        ok = (kpos <= pc[None, None, :, None, None])[:, :, None]
        w = masked_softmax(sc, ok, (-2, -1))
        return jnp.einsum('bgrqnk,bgqnkd->bqgrd', w.astype(vg.dtype), vg)

    o_sel = lax.map(sel_block, (q_blocks, idx_blocks, pos_blocks))
    o_sel = o_sel.swapaxes(0, 1).reshape(B, Tq, H, dh)
    return o_cmp, o_sel


def window_core(qg, k, v, q_pos, k_pos):
    s = jnp.einsum('btgrd,bkgd->bgrtk', qg, k).astype(jnp.float32) * (qg.shape[-1] ** -0.5)
    dist = q_pos[:, None] - k_pos[None, :]
    ok = (dist >= 0) & (dist < WINDOW) & (k_pos[None, :] >= 0)
    w = masked_softmax(s, ok, -1)
    return jnp.einsum('bgrtk,bkgd->btgrd', w.astype(v.dtype), v)


def window_banded(qg, k, v):
    B, T = qg.shape[:2]
    qb = QUERY_BLOCK if T % QUERY_BLOCK == 0 else T
    nqb = T // qb
    span = WINDOW + qb
    kidx = jnp.arange(nqb)[:, None] * qb + jnp.arange(span)[None, :]
    pad = ((0, 0), (WINDOW, 0), (0, 0), (0, 0))
    kb = jnp.pad(k, pad)[:, kidx]
    vb = jnp.pad(v, pad)[:, kidx]
    q_pos = jnp.arange(T).reshape(nqb, qb)
    k_pos = kidx - WINDOW
    q_blocks = qg.reshape(B, nqb, qb, *qg.shape[2:])
    o = jax.vmap(window_core, in_axes=(1, 1, 1, 0, 0), out_axes=1)(q_blocks, kb, vb, q_pos, k_pos)
    return o.reshape(qg.shape)


def mlstm_chunkwise(q, k, v, i_pre, f_pre, C0, n0, m0):
    B, T, H, d = q.shape
    L = MLSTM_CHUNK if T % MLSTM_CHUNK == 0 else T
    nc = T // L
    f32 = jnp.float32
    q, k, v = q.astype(f32), k.astype(f32) * (d ** -0.5), v.astype(f32)
    log_f = jax.nn.log_sigmoid(f_pre.astype(f32))
    i_pre = i_pre.astype(f32)
    chunks = lambda a: a.reshape(B, nc, L, *a.shape[2:]).swapaxes(0, 1)
    causal = jnp.tril(jnp.ones((L, L), dtype=bool))

    def step(carry, xs):
        C, n, m = carry
        qc, kc, vc, ic, fc = xs
        b = jnp.cumsum(fc, axis=1).swapaxes(1, 2)
        it = ic.swapaxes(1, 2)
        log_d = jnp.where(causal, b[..., :, None] - b[..., None, :] + it[..., None, :], -jnp.inf)
        m_inter = b + m[..., None]
        m_t = jnp.maximum(m_inter, jnp.max(log_d, axis=-1))
        w = jnp.einsum('blhd,bshd->bhls', qc, kc) * jnp.exp(log_d - m_t[..., None])
        carry_w = jnp.exp(m_inter - m_t)
        num = jnp.einsum('bhls,bshd->blhd', w, vc) + jnp.einsum('blhd,bhde->blhe', qc, C) * carry_w.swapaxes(1, 2)[..., None]
        den = jnp.sum(w, axis=-1) + jnp.einsum('blhd,bhd->bhl', qc, n) * carry_w
        den = jnp.maximum(jnp.abs(den), jnp.exp(-m_t))
        h = num / den.swapaxes(1, 2)[..., None]
        m_new = m_t[..., -1]
        w_end = jnp.exp(b[..., -1:] - b + it - m_new[..., None])
        decay = jnp.exp(b[..., -1] + m - m_new)
        C_new = decay[..., None, None] * C + jnp.einsum('bhs,bshd,bshe->bhde', w_end, kc, vc)
        n_new = decay[..., None] * n + jnp.einsum('bhs,bshd->bhd', w_end, kc)
        return (C_new, n_new, m_new), h

    (C, n, m), hs = lax.scan(step, (C0.astype(f32), n0.astype(f32), m0.astype(f32)),
                             tuple(chunks(a) for a in (q, k, v, i_pre, log_f)))
    return hs.swapaxes(0, 1).reshape(B, T, H, d), C, n, m


def nsa_mlstm_mixer(hn, start, past_rows, win_buf, C0, n0, m0, w_in0, nsa_cmp_pos, nsa_cmp_w,
                    mlstm_b_i, mlstm_b_f, mlstm_norm, w_out0):
    B, T, _ = hn.shape
    (q_a, k_c, v_c, k_s, v_s, k_w, v_w, g_a, q_m, k_m, v_m, o_m, i_m, f_m) = split_cols(hn @ w_in0, IN0_SIZES)
    kvh = lambda a: a.reshape(B, T, NSA_KV_HEADS, HEAD_DIM)
    mh = lambda a: a.reshape(B, T, MLSTM_HEADS, MLSTM_DIM)
    q_a = q_a.reshape(B, T, NSA_HEADS, HEAD_DIM)
    q_pos = start + jnp.arange(T)
    new_rows = jnp.stack([kvh(k_c), kvh(v_c), kvh(k_s), kvh(v_s)], axis=2)
    rows_full = jnp.concatenate([past_rows.astype(new_rows.dtype), new_rows], axis=1)
    o_cmp, o_sel = nsa_compressed_selected(q_a, rows_full, q_pos, nsa_cmp_pos, nsa_cmp_w)
    win_rows = jnp.stack([kvh(k_w), kvh(v_w)], axis=2)
    qg = q_a.reshape(B, T, NSA_KV_HEADS, NSA_GROUP, HEAD_DIM)
    if win_buf is None:
        o_win = window_banded(qg, win_rows[:, :, 0], win_rows[:, :, 1])
        new_win = win_rows[:, T - min(WINDOW, T):]
    else:
        n_buf = win_buf.shape[1]
        buf = jnp.concatenate([win_buf.astype(win_rows.dtype), win_rows], axis=1)
        k_pos = start - n_buf + jnp.arange(n_buf + T)
        o_win = window_core(qg, buf[:, :, 0], buf[:, :, 1], q_pos, k_pos)
        new_win = buf[:, T:]
    o_win = o_win.reshape(B, T, NSA_HEADS, HEAD_DIM)
    gates = jax.nn.sigmoid(g_a.reshape(B, T, NSA_HEADS, 3))
    o_a = gates[..., 0:1] * o_cmp + gates[..., 1:2] * o_sel + gates[..., 2:3] * o_win
    h_m, C, n, m = mlstm_chunkwise(mh(q_m), mh(k_m), mh(v_m), i_m + mlstm_b_i, f_m + mlstm_b_f, C0, n0, m0)
    h_m = head_rmsnorm(h_m, mlstm_norm, hn.dtype) * jax.nn.sigmoid(mh(o_m))
    mixed = jnp.concatenate([o_a.reshape(B, T, -1), h_m.reshape(B, T, -1)], axis=-1)
    return mixed @ w_out0, new_rows, new_win, C, n, m


def sb_attend(q, k, v, start):
    B, Tq, H, dh = q.shape
    Tk = k.shape[1]
    qb = QUERY_BLOCK if Tq % QUERY_BLOCK == 0 else Tq
    nqb = Tq // qb
    q_blocks = q.reshape(B, nqb, qb, H, dh).swapaxes(0, 1)
    pos_blocks = (start + jnp.arange(Tq)).reshape(nqb, qb)
    k_pos = jnp.arange(Tk)
    scale = dh ** -0.5

    def block(args):
        qc, pc = args
        z = jnp.einsum('bqhd,bkhd->bhqk', qc, k).astype(jnp.float32) * scale
        before = k_pos[None, :] < pc[:, None]
        log_keep = jnp.where(before, -jax.nn.softplus(z), 0.0)
        tail = lax.cumsum(log_keep, axis=3, reverse=True) - log_keep
        a = jnp.where(before, jnp.exp(jax.nn.log_sigmoid(z) + tail), 0.0)
        return jnp.einsum('bhqk,bkhd->bqhd', a.astype(v.dtype), v)

    o = lax.map(block, (q_blocks, pos_blocks))
    return o.swapaxes(0, 1).reshape(B, Tq, H, dh)


def stick_breaking_mixer(hn, start, past_k, past_v, w_qkv1, w_out1):
    B, T, _ = hn.shape
    qkv = (hn @ w_qkv1).reshape(B, T, 3, SB_HEADS, HEAD_DIM)
    k_full = jnp.concatenate([past_k.astype(qkv.dtype), qkv[:, :, 1]], axis=1)
    v_full = jnp.concatenate([past_v.astype(qkv.dtype), qkv[:, :, 2]], axis=1)
    o = sb_attend(qkv[:, :, 0], k_full, v_full, start)
    return o.reshape(B, T, -1) @ w_out1, qkv[:, :, 1:]


def setup_inputs(seed: int = 0) -> dict:
    key = jax.random.key(seed)
    ks = jax.random.split(key, 32)
    nrm = lambda k, shape, s: jax.random.normal(k, shape, jnp.float32) * s
    gain = lambda k, shape: 1.0 + nrm(k, shape, 0.02)
    n_pages = PAST_LEN // PAGE_SIZE
    n_pool = (DEC_BATCH * n_pages * 5) // 4
    n_buf = min(WINDOW, PAST_LEN)
    page_table = jax.random.permutation(ks[0], n_pool)[: DEC_BATCH * n_pages].reshape(DEC_BATCH, n_pages).astype(jnp.int32)
    D = D_MODEL
    return {
        'x_prompt': nrm(ks[1], (BATCH, SEQ, D), 1.0),
        'x_sample': nrm(ks[2], (DEC_BATCH, DEC_SEQ, D), 1.0),
        'cache_nsa_kv': nrm(ks[3], (n_pool, PAGE_SIZE, 4, NSA_KV_HEADS, HEAD_DIM), 1.0),
        'state_nsa_win': nrm(ks[4], (DEC_BATCH, n_buf, 2, NSA_KV_HEADS, HEAD_DIM), 1.0),
        'state_mlstm_C': nrm(ks[5], (DEC_BATCH, MLSTM_HEADS, MLSTM_DIM, MLSTM_DIM), 0.1),
        'state_mlstm_n': nrm(ks[6], (DEC_BATCH, MLSTM_HEADS, MLSTM_DIM), 0.1),
        'state_mlstm_m': nrm(ks[7], (DEC_BATCH, MLSTM_HEADS), 1.0),
        'cache_sb_kv': nrm(ks[8], (n_pool, PAGE_SIZE, 2, SB_HEADS, HEAD_DIM), 1.0),
        'page_table': page_table,
        'norm_mix0': gain(ks[9], (D,)),
        'w_in0': nrm(ks[10], (D, IN0_COLS), D ** -0.5),
        'nsa_cmp_pos': nrm(ks[11], (2, CMP_BLOCK, HEAD_DIM), 0.1),
        'nsa_cmp_w': nrm(ks[12], (2, CMP_BLOCK * HEAD_DIM, HEAD_DIM), (CMP_BLOCK * HEAD_DIM) ** -0.5),
        'mlstm_b_i': nrm(ks[13], (MLSTM_HEADS,), 0.1),
        'mlstm_b_f': jnp.linspace(3.0, 6.0, MLSTM_HEADS, dtype=jnp.float32) + nrm(ks[14], (MLSTM_HEADS,), 0.1),
        'mlstm_norm': gain(ks[15], (MLSTM_HEADS, MLSTM_DIM)),
        'w_out0': nrm(ks[16], (MIX0_WIDTH, D), MIX0_WIDTH ** -0.5),
        'norm_ffn0': gain(ks[17], (D,)),
        'w_gate0': nrm(ks[18], (D, D_FF), D ** -0.5),
        'w_up0': nrm(ks[19], (D, D_FF), D ** -0.5),
        'w_down0': nrm(ks[20], (D_FF, D), D_FF ** -0.5),
        'norm_mix1': gain(ks[21], (D,)),
        'w_qkv1': nrm(ks[22], (D, 3 * SB_HEADS * HEAD_DIM), D ** -0.5),
        'w_out1': nrm(ks[23], (SB_HEADS * HEAD_DIM, D), (SB_HEADS * HEAD_DIM) ** -0.5),
        'norm_ffn1': gain(ks[24], (D,)),
        'w_router1': nrm(ks[25], (D, N_EXPERTS), D ** -0.5),
        'w_gate1': nrm(ks[26], (N_EXPERTS, D, D_FF_EXPERT), D ** -0.5),
        'w_up1': nrm(ks[27], (N_EXPERTS, D, D_FF_EXPERT), D ** -0.5),
        'w_down1': nrm(ks[28], (N_EXPERTS, D_FF_EXPERT, D), D_FF_EXPERT ** -0.5),
        'norm_final': gain(ks[29], (D,)),
    }


def reference(x_prompt, x_sample, cache_nsa_kv, state_nsa_win, state_mlstm_C, state_mlstm_n, state_mlstm_m,
              cache_sb_kv, page_table, norm_mix0, w_in0, nsa_cmp_pos, nsa_cmp_w, mlstm_b_i, mlstm_b_f,
              mlstm_norm, w_out0, norm_ffn0, w_gate0, w_up0, w_down0, norm_mix1, w_qkv1, w_out1, norm_ffn1,
              w_router1, w_gate1, w_up1, w_down1, norm_final):
    B = x_prompt.shape[0]
    DB = x_sample.shape[0]
    past_len = page_table.shape[1] * cache_nsa_kv.shape[1]
    xp, xs = x_prompt, x_sample
    for layer in range(DEPTH):
        if layer % 2 == 0:
            empty = jnp.zeros((B, 0, 4, NSA_KV_HEADS, HEAD_DIM), xp.dtype)
            c0 = jnp.zeros((B, MLSTM_HEADS, MLSTM_DIM, MLSTM_DIM), jnp.float32)
            n0 = jnp.zeros((B, MLSTM_HEADS, MLSTM_DIM), jnp.float32)
            m0 = jnp.zeros((B, MLSTM_HEADS), jnp.float32)
            yp, nsa_rows_p, nsa_win_p, mC_p, mn_p, mm_p = nsa_mlstm_mixer(
                rmsnorm(xp, norm_mix0), 0, empty, None, c0, n0, m0, w_in0, nsa_cmp_pos, nsa_cmp_w,
                mlstm_b_i, mlstm_b_f, mlstm_norm, w_out0)
            past_rows = cache_nsa_kv[page_table].reshape(DB, past_len, 4, NSA_KV_HEADS, HEAD_DIM)
            ys, nsa_rows_s, nsa_win_s, mC_s, mn_s, mm_s = nsa_mlstm_mixer(
                rmsnorm(xs, norm_mix0), past_len, past_rows, state_nsa_win, state_mlstm_C, state_mlstm_n,
                state_mlstm_m, w_in0, nsa_cmp_pos, nsa_cmp_w, mlstm_b_i, mlstm_b_f, mlstm_norm, w_out0)
            xp, xs = xp + yp, xs + ys
            xp = xp + swiglu(rmsnorm(xp, norm_ffn0), w_gate0, w_up0, w_down0)
            xs = xs + swiglu(rmsnorm(xs, norm_ffn0), w_gate0, w_up0, w_down0)
        else:
            empty = jnp.zeros((B, 0, SB_HEADS, HEAD_DIM), xp.dtype)
            yp, sb_rows_p = stick_breaking_mixer(rmsnorm(xp, norm_mix1), 0, empty, empty, w_qkv1, w_out1)
            past_k = cache_sb_kv[page_table, :, 0].reshape(DB, past_len, SB_HEADS, HEAD_DIM)
            past_v = cache_sb_kv[page_table, :, 1].reshape(DB, past_len, SB_HEADS, HEAD_DIM)
            ys, sb_rows_s = stick_breaking_mixer(rmsnorm(xs, norm_mix1), past_len, past_k, past_v, w_qkv1, w_out1)
            xp, xs = xp + yp, xs + ys
            xp = xp + moe_swiglu(rmsnorm(xp, norm_ffn1), w_router1, w_gate1, w_up1, w_down1)
            xs = xs + moe_swiglu(rmsnorm(xs, norm_ffn1), w_router1, w_gate1, w_up1, w_down1)
    y_prompt = rmsnorm(xp, norm_final)
    y_sample = rmsnorm(xs, norm_final)
    return (y_prompt, y_sample, nsa_rows_p, nsa_rows_s, nsa_win_p, nsa_win_s, mC_p, mn_p, mm_p, mC_s, mn_s, mm_s, sb_rows_p, sb_rows_s)
```

```python
import functools

import numpy as np
import jax
import jax.numpy as jnp
from jax import lax
from jax.experimental import pallas as pl
from jax.experimental.pallas import tpu as pltpu

D_MODEL = 1024
HEAD_DIM = 64
NSA_HEADS = 8
NSA_KV_HEADS = 2
NSA_GROUP = NSA_HEADS // NSA_KV_HEADS
CMP_BLOCK = 32
SEL_BLOCK = 64
TOP_N = 16
WINDOW = 512
SEL_QUERY_BLOCK = 64
FORCED_SCORE = 1e4
MLSTM_HEADS = 4
MLSTM_DIM = 128
MLSTM_CHUNK = 64
SB_HEADS = D_MODEL // HEAD_DIM
QUERY_BLOCK = 128
N_EXPERTS = 8
TOP_K = 2
EPS = 1e-6
NEG_BIG = -1e30
TINY = 1e-30

IN0_SIZES = (NSA_HEADS * HEAD_DIM,) + (NSA_KV_HEADS * HEAD_DIM,) * 6 + (3 * NSA_HEADS,) + (MLSTM_HEADS * MLSTM_DIM,) * 4 + (MLSTM_HEADS, MLSTM_HEADS)
IN0_COLS = sum(IN0_SIZES)

LANES = 128
TOKEN_TILE = 512
VMEM_LIMIT = 48 * 1024 * 1024
BF16 = jnp.bfloat16
F32 = jnp.float32


def _cparams(sem):
    return pltpu.CompilerParams(dimension_semantics=sem, vmem_limit_bytes=VMEM_LIMIT)


def _rmsnorm_kernel(x_ref, g_ref, o_ref):
    x = x_ref[...]
    y = x * lax.rsqrt(jnp.mean(x * x, axis=-1, keepdims=True) + EPS)
    o_ref[...] = (y * g_ref[...]).astype(o_ref.dtype)


def rmsnorm_tokens(x, g, out_dtype):
    n, d = x.shape
    return pl.pallas_call(
        _rmsnorm_kernel,
        grid=(n // TOKEN_TILE,),
        in_specs=[pl.BlockSpec((TOKEN_TILE, d), lambda i: (i, 0)),
                  pl.BlockSpec((1, d), lambda i: (0, 0))],
        out_specs=pl.BlockSpec((TOKEN_TILE, d), lambda i: (i, 0)),
        out_shape=jax.ShapeDtypeStruct((n, d), out_dtype),
        compiler_params=_cparams(("parallel",)),
        name="rmsnorm",
    )(x, g.reshape(1, d))


def _rmsnorm_router_kernel(x_ref, g_ref, wr_ref, o_ref, logit_ref):
    x = x_ref[...]
    y = x * lax.rsqrt(jnp.mean(x * x, axis=-1, keepdims=True) + EPS) * g_ref[...]
    o_ref[...] = y.astype(o_ref.dtype)
    logit_ref[...] = jnp.dot(y, wr_ref[...], precision=lax.Precision.HIGHEST,
                             preferred_element_type=F32)


def rmsnorm_router(x, g, w_router):
    n, d = x.shape
    wr = jnp.pad(w_router, ((0, 0), (0, LANES - w_router.shape[1])))
    return pl.pallas_call(
        _rmsnorm_router_kernel,
        grid=(n // TOKEN_TILE,),
        in_specs=[pl.BlockSpec((TOKEN_TILE, d), lambda i: (i, 0)),
                  pl.BlockSpec((1, d), lambda i: (0, 0)),
                  pl.BlockSpec((d, LANES), lambda i: (0, 0))],
        out_specs=[pl.BlockSpec((TOKEN_TILE, d), lambda i: (i, 0)),
                   pl.BlockSpec((TOKEN_TILE, LANES), lambda i: (i, 0))],
        out_shape=[jax.ShapeDtypeStruct((n, d), BF16),
                   jax.ShapeDtypeStruct((n, LANES), F32)],
        compiler_params=_cparams(("parallel",)),
        name="rmsnorm_router",
    )(x, g.reshape(1, d), wr)


def _matmul_kernel(a_ref, w_ref, o_ref):
    o_ref[...] = jnp.dot(a_ref[...], w_ref[...], preferred_element_type=F32)


def _matmul_res_kernel(a_ref, w_ref, r_ref, o_ref):
    o_ref[...] = r_ref[...] + jnp.dot(a_ref[...], w_ref[...], preferred_element_type=F32)


def matmul(a, w, res=None, tn=1024):
    n, k = a.shape
    m = w.shape[1]
    tn = min(tn, m)
    assert n % TOKEN_TILE == 0 and m % tn == 0
    in_specs = [pl.BlockSpec((TOKEN_TILE, k), lambda i, j: (i, 0)),
                pl.BlockSpec((k, tn), lambda i, j: (0, j))]
    args = [a, w]
    body = _matmul_kernel
    if res is not None:
        in_specs.append(pl.BlockSpec((TOKEN_TILE, tn), lambda i, j: (i, j)))
        args.append(res)
        body = _matmul_res_kernel
    return pl.pallas_call(
        body,
        grid=(n // TOKEN_TILE, m // tn),
        in_specs=in_specs,
        out_specs=pl.BlockSpec((TOKEN_TILE, tn), lambda i, j: (i, j)),
        out_shape=jax.ShapeDtypeStruct((n, m), F32),
        compiler_params=_cparams(("parallel", "parallel")),
        name="matmul",
    )(*args)


def _swiglu_kernel(te_ref, nv_ref, x_ref, wg_ref, wu_ref, wd_ref, cw_ref, *rest, has_res):
    if has_res:
        r_ref, o_ref, acc_ref = rest
    else:
        o_ref, acc_ref = rest
    i = pl.program_id(0)
    j = pl.program_id(1)

    @pl.when(j == 0)
    def _():
        acc_ref[...] = jnp.zeros_like(acc_ref)

    @pl.when(i < nv_ref[0])
    def _():
        x = x_ref[...]
        g = jnp.dot(x, wg_ref[0], preferred_element_type=F32)
        u = jnp.dot(x, wu_ref[0], preferred_element_type=F32)
        h = (g * jax.nn.sigmoid(g)) * u
        acc_ref[...] += jnp.dot(h.astype(BF16), wd_ref[0], preferred_element_type=F32)

    @pl.when(j == pl.num_programs(1) - 1)
    def _():
        y = acc_ref[...] * cw_ref[...]
        if has_res:
            y = y + r_ref[...]
        o_ref[...] = y


def grouped_swiglu(x, wg, wu, wd, tile_expert, n_valid, cw, res, tf):
    r, d = x.shape
    f = wg.shape[2]
    assert r % TOKEN_TILE == 0 and f % tf == 0
    has_res = res is not None
    in_specs = [
        pl.BlockSpec((TOKEN_TILE, d), lambda i, j, te, nv: (i, 0)),
        pl.BlockSpec((1, d, tf), lambda i, j, te, nv: (te[i], 0, j)),
        pl.BlockSpec((1, d, tf), lambda i, j, te, nv: (te[i], 0, j)),
        pl.BlockSpec((1, tf, d), lambda i, j, te, nv: (te[i], j, 0)),
        pl.BlockSpec((TOKEN_TILE, 1), lambda i, j, te, nv: (i, 0)),
    ]
    args = [x, wg, wu, wd, cw]
    if has_res:
        in_specs.append(pl.BlockSpec((TOKEN_TILE, d), lambda i, j, te, nv: (i, 0)))
        args.append(res)
    grid_spec = pltpu.PrefetchScalarGridSpec(
        num_scalar_prefetch=2,
        grid=(r // TOKEN_TILE, f // tf),
        in_specs=in_specs,
        out_specs=pl.BlockSpec((TOKEN_TILE, d), lambda i, j, te, nv: (i, 0)),
        scratch_shapes=[pltpu.VMEM((TOKEN_TILE, d), F32)],
    )
    return pl.pallas_call(
        functools.partial(_swiglu_kernel, has_res=has_res),
        grid_spec=grid_spec,
        out_shape=jax.ShapeDtypeStruct((r, d), F32),
        compiler_params=_cparams(("parallel", "arbitrary")),
        name="grouped_swiglu",
    )(tile_expert, n_valid, *args)


def dense_ffn(xn, res, wg, wu, wd):
    n = xn.shape[0]
    tiles = n // TOKEN_TILE
    return grouped_swiglu(xn, wg[None].astype(BF16), wu[None].astype(BF16), wd[None].astype(BF16),
                          jnp.zeros((tiles,), jnp.int32), jnp.full((1,), tiles, jnp.int32),
                          jnp.ones((n, 1), F32), res, tf=256)


def moe_ffn(xn, logits, res, wg, wu, wd):
    n, d = xn.shape
    probs = jax.nn.softmax(logits, axis=-1)
    top_w, top_i = lax.top_k(probs, TOP_K)
    top_w = top_w / jnp.sum(top_w, axis=-1, keepdims=True)
    flat_e = top_i.reshape(-1)
    order = jnp.argsort(flat_e, stable=True)
    sorted_e = flat_e[order]
    counts = jnp.sum(jax.nn.one_hot(flat_e, N_EXPERTS, dtype=jnp.int32), axis=0)
    padded = ((counts + TOKEN_TILE - 1) // TOKEN_TILE) * TOKEN_TILE
    ends_p = jnp.cumsum(padded)
    starts_p = ends_p - padded
    starts = jnp.cumsum(counts) - counts
    s_ix = jnp.arange(TOP_K * n, dtype=jnp.int32)
    dest = starts_p[sorted_e] + (s_ix - starts[sorted_e])
    rows = TOP_K * n + N_EXPERTS * TOKEN_TILE
    src_tok = jnp.zeros((rows,), jnp.int32).at[dest].set(order // TOP_K)
    cw = jnp.zeros((rows,), F32).at[dest].set(top_w.reshape(-1)[order])
    pos = jnp.zeros((TOP_K * n,), jnp.int32).at[order].set(dest)
    tiles = rows // TOKEN_TILE
    tile_start = jnp.arange(tiles, dtype=jnp.int32) * TOKEN_TILE
    tile_expert = jnp.minimum(jnp.searchsorted(ends_p, tile_start, side="right"), N_EXPERTS - 1).astype(jnp.int32)
    n_valid = (ends_p[-1] // TOKEN_TILE).astype(jnp.int32).reshape(1)
    x_sorted = xn[src_tok]
    y_sorted = grouped_swiglu(x_sorted, wg.astype(BF16), wu.astype(BF16), wd.astype(BF16),
                              tile_expert, n_valid, cw.reshape(rows, 1), None, tf=512)
    pos = pos.reshape(n, TOP_K)
    return res + y_sorted[pos[:, 0]] + y_sorted[pos[:, 1]]


def head_rmsnorm(h, g, dtype):
    y = h * lax.rsqrt(jnp.mean(h * h, axis=-1, keepdims=True) + EPS)
    return (y * g.astype(jnp.float32)).astype(dtype)


def masked_softmax(s, valid, axes):
    s = jnp.where(valid, s, NEG_BIG)
    m = jnp.max(s, axis=axes, keepdims=True)
    e = jnp.where(valid, jnp.exp(s - m), 0.0)
    return e / jnp.maximum(jnp.sum(e, axis=axes, keepdims=True), TINY)


def split_cols(h, sizes):
    offs = [int(o) for o in np.cumsum(sizes)[:-1]]
    return jnp.split(h, offs, axis=-1)


def nsa_compress(rows, pos_emb, w):
    B, Tk, G, dh = rows.shape
    blk = rows.reshape(B, Tk // CMP_BLOCK, CMP_BLOCK, G, dh) + pos_emb[None, None, :, None, :]
    blk = blk.transpose(0, 1, 3, 2, 4).reshape(B, Tk // CMP_BLOCK, G, CMP_BLOCK * dh)
    return blk @ w


def nsa_compressed_selected(q, rows, q_pos, cmp_pos, cmp_w):
    B, Tq, H, dh = q.shape
    G, R = NSA_KV_HEADS, NSA_GROUP
    pad = (-rows.shape[1]) % SEL_BLOCK
    rows = jnp.pad(rows, ((0, 0), (0, pad), (0, 0), (0, 0), (0, 0)))
    Tp = rows.shape[1]
    scale = dh ** -0.5
    qg = q.reshape(B, Tq, G, R, dh)
    k_cmp = nsa_compress(rows[:, :, 0], cmp_pos[0], cmp_w[0])
    v_cmp = nsa_compress(rows[:, :, 1], cmp_pos[1], cmp_w[1])
    n_cmp = Tp // CMP_BLOCK
    s = jnp.einsum('btgrd,bngd->bgrtn', qg, k_cmp).astype(jnp.float32) * scale
    cmp_end = (jnp.arange(n_cmp) + 1) * CMP_BLOCK - 1
    p = masked_softmax(s, cmp_end[None, :] <= q_pos[:, None], -1)
    o_cmp = jnp.einsum('bgrtn,bngd->btgrd', p.astype(v_cmp.dtype), v_cmp).reshape(B, Tq, H, dh)
    n_sel = Tp // SEL_BLOCK
    imp = p.sum(axis=2).reshape(B, G, Tq, n_sel, SEL_BLOCK // CMP_BLOCK).sum(axis=-1)
    blk = jnp.arange(n_sel)
    forced = (blk[None, :] == 0) | (blk[None, :] == (q_pos // SEL_BLOCK)[:, None])
    avail = blk[None, :] * SEL_BLOCK <= q_pos[:, None]
    imp = jnp.where(avail, jnp.where(forced, FORCED_SCORE, imp), -1.0)
    _, idx = lax.top_k(imp, min(TOP_N, n_sel))
    n_top = idx.shape[-1]
    ks = rows[:, :, 2].reshape(B, n_sel, SEL_BLOCK, G, dh).transpose(0, 3, 1, 2, 4)
    vs = rows[:, :, 3].reshape(B, n_sel, SEL_BLOCK, G, dh).transpose(0, 3, 1, 2, 4)
    qb = SEL_QUERY_BLOCK if Tq % SEL_QUERY_BLOCK == 0 else Tq
    nqb = Tq // qb
    q_blocks = qg.reshape(B, nqb, qb, G, R, dh).swapaxes(0, 1)
    idx_blocks = idx.reshape(B, G, nqb, qb, n_top).transpose(2, 0, 1, 3, 4)
    pos_blocks = q_pos.reshape(nqb, qb)
    b_ix = jnp.arange(B)[:, None, None, None]
    g_ix = jnp.arange(G)[None, :, None, None]

    def sel_block(args):
        qc, ic, pc = args
        kg = ks[b_ix, g_ix, ic]
        vg = vs[b_ix, g_ix, ic]
        sc = jnp.einsum('bqgrd,bgqnkd->bgrqnk', qc, kg).astype(jnp.float32) * scale
        kpos = ic[..., None] * SEL_BLOCK + jnp.arange(SEL_BLOCK)
        ok = (kpos <= pc[None, None, :, None, None])[:, :, None]
        w = masked_softmax(sc, ok, (-2, -1))
        return jnp.einsum('bgrqnk,bgqnkd->bqgrd', w.astype(vg.dtype), vg)

    o_sel = lax.map(sel_block, (q_blocks, idx_blocks, pos_blocks))
    o_sel = o_sel.swapaxes(0, 1).reshape(B, Tq, H, dh)
    return o_cmp, o_sel


def window_core(qg, k, v, q_pos, k_pos):
    s = jnp.einsum('btgrd,bkgd->bgrtk', qg, k).astype(jnp.float32) * (qg.shape[-1] ** -0.5)
    dist = q_pos[:, None] - k_pos[None, :]
    ok = (dist >= 0) & (dist < WINDOW) & (k_pos[None, :] >= 0)
    w = masked_softmax(s, ok, -1)
    return jnp.einsum('bgrtk,bkgd->btgrd', w.astype(v.dtype), v)


def window_banded(qg, k, v):
    B, T = qg.shape[:2]
    qb = QUERY_BLOCK if T % QUERY_BLOCK == 0 else T
    nqb = T // qb
    span = WINDOW + qb
    kidx = jnp.arange(nqb)[:, None] * qb + jnp.arange(span)[None, :]
    pad = ((0, 0), (WINDOW, 0), (0, 0), (0, 0))
    kb = jnp.pad(k, pad)[:, kidx]
    vb = jnp.pad(v, pad)[:, kidx]
    q_pos = jnp.arange(T).reshape(nqb, qb)
    k_pos = kidx - WINDOW
    q_blocks = qg.reshape(B, nqb, qb, *qg.shape[2:])
    o = jax.vmap(window_core, in_axes=(1, 1, 1, 0, 0), out_axes=1)(q_blocks, kb, vb, q_pos, k_pos)
    return o.reshape(qg.shape)


def mlstm_chunkwise(q, k, v, i_pre, f_pre, C0, n0, m0):
    B, T, H, d = q.shape
    L = MLSTM_CHUNK if T % MLSTM_CHUNK == 0 else T
    nc = T // L
    f32 = jnp.float32
    q, k, v = q.astype(f32), k.astype(f32) * (d ** -0.5), v.astype(f32)
    log_f = jax.nn.log_sigmoid(f_pre.astype(f32))
    i_pre = i_pre.astype(f32)
    chunks = lambda a: a.reshape(B, nc, L, *a.shape[2:]).swapaxes(0, 1)
    causal = jnp.tril(jnp.ones((L, L), dtype=bool))

    def step(carry, xs):
        C, n, m = carry
        qc, kc, vc, ic, fc = xs
        b = jnp.cumsum(fc, axis=1).swapaxes(1, 2)
        it = ic.swapaxes(1, 2)
        log_d = jnp.where(causal, b[..., :, None] - b[..., None, :] + it[..., None, :], -jnp.inf)
        m_inter = b + m[..., None]
        m_t = jnp.maximum(m_inter, jnp.max(log_d, axis=-1))
        w = jnp.einsum('blhd,bshd->bhls', qc, kc) * jnp.exp(log_d - m_t[..., None])
        carry_w = jnp.exp(m_inter - m_t)
        num = jnp.einsum('bhls,bshd->blhd', w, vc) + jnp.einsum('blhd,bhde->blhe', qc, C) * carry_w.swapaxes(1, 2)[..., None]
        den = jnp.sum(w, axis=-1) + jnp.einsum('blhd,bhd->bhl', qc, n) * carry_w
        den = jnp.maximum(jnp.abs(den), jnp.exp(-m_t))
        h = num / den.swapaxes(1, 2)[..., None]
        m_new = m_t[..., -1]
        w_end = jnp.exp(b[..., -1:] - b + it - m_new[..., None])
        decay = jnp.exp(b[..., -1] + m - m_new)
        C_new = decay[..., None, None] * C + jnp.einsum('bhs,bshd,bshe->bhde', w_end, kc, vc)
        n_new = decay[..., None] * n + jnp.einsum('bhs,bshd->bhd', w_end, kc)
        return (C_new, n_new, m_new), h

    (C, n, m), hs = lax.scan(step, (C0.astype(f32), n0.astype(f32), m0.astype(f32)),
                             tuple(chunks(a) for a in (q, k, v, i_pre, log_f)))
    return hs.swapaxes(0, 1).reshape(B, T, H, d), C, n, m


def nsa_mlstm_core(proj, start, past_rows, win_buf, C0, n0, m0, nsa_cmp_pos, nsa_cmp_w,
                   mlstm_b_i, mlstm_b_f, mlstm_norm):
    B, T, _ = proj.shape
    (q_a, k_c, v_c, k_s, v_s, k_w, v_w, g_a, q_m, k_m, v_m, o_m, i_m, f_m) = split_cols(proj, IN0_SIZES)
    kvh = lambda a: a.reshape(B, T, NSA_KV_HEADS, HEAD_DIM)
    mh = lambda a: a.reshape(B, T, MLSTM_HEADS, MLSTM_DIM)
    q_a = q_a.reshape(B, T, NSA_HEADS, HEAD_DIM)
    q_pos = start + jnp.arange(T)
    new_rows = jnp.stack([kvh(k_c), kvh(v_c), kvh(k_s), kvh(v_s)], axis=2)
    rows_full = jnp.concatenate([past_rows.astype(new_rows.dtype), new_rows], axis=1)
    o_cmp, o_sel = nsa_compressed_selected(q_a, rows_full, q_pos, nsa_cmp_pos, nsa_cmp_w)
    win_rows = jnp.stack([kvh(k_w), kvh(v_w)], axis=2)
    qg = q_a.reshape(B, T, NSA_KV_HEADS, NSA_GROUP, HEAD_DIM)
    if win_buf is None:
        o_win = window_banded(qg, win_rows[:, :, 0], win_rows[:, :, 1])
        new_win = win_rows[:, T - min(WINDOW, T):]
    else:
        n_buf = win_buf.shape[1]
        buf = jnp.concatenate([win_buf.astype(win_rows.dtype), win_rows], axis=1)
        k_pos = start - n_buf + jnp.arange(n_buf + T)
        o_win = window_core(qg, buf[:, :, 0], buf[:, :, 1], q_pos, k_pos)
        new_win = buf[:, T:]
    o_win = o_win.reshape(B, T, NSA_HEADS, HEAD_DIM)
    gates = jax.nn.sigmoid(g_a.reshape(B, T, NSA_HEADS, 3))
    o_a = gates[..., 0:1] * o_cmp + gates[..., 1:2] * o_sel + gates[..., 2:3] * o_win
    h_m, C, n, m = mlstm_chunkwise(mh(q_m), mh(k_m), mh(v_m), i_m + mlstm_b_i, f_m + mlstm_b_f, C0, n0, m0)
    h_m = head_rmsnorm(h_m, mlstm_norm, proj.dtype) * jax.nn.sigmoid(mh(o_m))
    mixed = jnp.concatenate([o_a.reshape(B, T, -1), h_m.reshape(B, T, -1)], axis=-1)
    return mixed, new_rows, new_win, C, n, m


def sb_attend(q, k, v, start):
    B, Tq, H, dh = q.shape
    Tk = k.shape[1]
    qb = QUERY_BLOCK if Tq % QUERY_BLOCK == 0 else Tq
    nqb = Tq // qb
    q_blocks = q.reshape(B, nqb, qb, H, dh).swapaxes(0, 1)
    pos_blocks = (start + jnp.arange(Tq)).reshape(nqb, qb)
    k_pos = jnp.arange(Tk)
    scale = dh ** -0.5

    def block(args):
        qc, pc = args
        z = jnp.einsum('bqhd,bkhd->bhqk', qc, k).astype(jnp.float32) * scale
        before = k_pos[None, :] < pc[:, None]
        log_keep = jnp.where(before, -jax.nn.softplus(z), 0.0)
        tail = lax.cumsum(log_keep, axis=3, reverse=True) - log_keep
        a = jnp.where(before, jnp.exp(jax.nn.log_sigmoid(z) + tail), 0.0)
        return jnp.einsum('bhqk,bkhd->bqhd', a.astype(v.dtype), v)

    o = lax.map(block, (q_blocks, pos_blocks))
    return o.swapaxes(0, 1).reshape(B, Tq, H, dh)


def sb_core(qkv, start, past_k, past_v):
    k_full = jnp.concatenate([past_k.astype(qkv.dtype), qkv[:, :, 1]], axis=1)
    v_full = jnp.concatenate([past_v.astype(qkv.dtype), qkv[:, :, 2]], axis=1)
    o = sb_attend(qkv[:, :, 0], k_full, v_full, start)
    return o.reshape(o.shape[0], o.shape[1], -1), qkv[:, :, 1:]


def kernel(x_prompt, x_sample, cache_nsa_kv, state_nsa_win, state_mlstm_C, state_mlstm_n, state_mlstm_m, cache_sb_kv, page_table, norm_mix0, w_in0, nsa_cmp_pos, nsa_cmp_w, mlstm_b_i, mlstm_b_f, mlstm_norm, w_out0, norm_ffn0, w_gate0, w_up0, w_down0, norm_mix1, w_qkv1, w_out1, norm_ffn1, w_router1, w_gate1, w_up1, w_down1, norm_final):
    B, T, D = x_prompt.shape
    DB, TS, _ = x_sample.shape
    n_p = B * T
    past_len = page_table.shape[1] * cache_nsa_kv.shape[1]
    x = jnp.concatenate([x_prompt.reshape(n_p, D), x_sample.reshape(DB * TS, D)], axis=0)

    hn = rmsnorm_tokens(x, norm_mix0, BF16)
    in0_pad = (-IN0_COLS) % (9 * LANES)
    w_in = jnp.pad(w_in0, ((0, 0), (0, in0_pad))).astype(BF16)
    proj = matmul(hn, w_in, tn=(IN0_COLS + in0_pad) // 3)[:, :IN0_COLS]
    proj_p = proj[:n_p].reshape(B, T, IN0_COLS)
    proj_s = proj[n_p:].reshape(DB, TS, IN0_COLS)
    empty = jnp.zeros((B, 0, 4, NSA_KV_HEADS, HEAD_DIM), F32)
    c0 = jnp.zeros((B, MLSTM_HEADS, MLSTM_DIM, MLSTM_DIM), F32)
    n0 = jnp.zeros((B, MLSTM_HEADS, MLSTM_DIM), F32)
    m0 = jnp.zeros((B, MLSTM_HEADS), F32)
    mixed_p, nsa_rows_p, nsa_win_p, mC_p, mn_p, mm_p = nsa_mlstm_core(
        proj_p, 0, empty, None, c0, n0, m0, nsa_cmp_pos, nsa_cmp_w, mlstm_b_i, mlstm_b_f, mlstm_norm)
    past_rows = cache_nsa_kv[page_table].reshape(DB, past_len, 4, NSA_KV_HEADS, HEAD_DIM)
    mixed_s, nsa_rows_s, nsa_win_s, mC_s, mn_s, mm_s = nsa_mlstm_core(
        proj_s, past_len, past_rows, state_nsa_win, state_mlstm_C, state_mlstm_n, state_mlstm_m,
        nsa_cmp_pos, nsa_cmp_w, mlstm_b_i, mlstm_b_f, mlstm_norm)
    mixed = jnp.concatenate([mixed_p.reshape(n_p, D), mixed_s.reshape(DB * TS, D)], axis=0)
    x = matmul(mixed.astype(BF16), w_out0.astype(BF16), res=x)
    x = dense_ffn(rmsnorm_tokens(x, norm_ffn0, BF16), x, w_gate0, w_up0, w_down0)

    hn = rmsnorm_tokens(x, norm_mix1, BF16)
    qkv = matmul(hn, w_qkv1.astype(BF16))
    qkv_p = qkv[:n_p].reshape(B, T, 3, SB_HEADS, HEAD_DIM)
    qkv_s = qkv[n_p:].reshape(DB, TS, 3, SB_HEADS, HEAD_DIM)
    empty = jnp.zeros((B, 0, SB_HEADS, HEAD_DIM), F32)
    o_p, sb_rows_p = sb_core(qkv_p, 0, empty, empty)
    past_k = cache_sb_kv[page_table, :, 0].reshape(DB, past_len, SB_HEADS, HEAD_DIM)
    past_v = cache_sb_kv[page_table, :, 1].reshape(DB, past_len, SB_HEADS, HEAD_DIM)
    o_s, sb_rows_s = sb_core(qkv_s, past_len, past_k, past_v)
    o = jnp.concatenate([o_p.reshape(n_p, D), o_s.reshape(DB * TS, D)], axis=0)
    x = matmul(o.astype(BF16), w_out1.astype(BF16), res=x)
    hn, logits = rmsnorm_router(x, norm_ffn1, w_router1)
    x = moe_ffn(hn, logits[:, :N_EXPERTS], x, w_gate1, w_up1, w_down1)

    y = rmsnorm_tokens(x, norm_final, F32)
    y_prompt = y[:n_p].reshape(B, T, D)
    y_sample = y[n_p:].reshape(DB, TS, D)
    return (y_prompt, y_sample, nsa_rows_p, nsa_rows_s, nsa_win_p, nsa_win_s, mC_p, mn_p, mm_p,
            mC_s, mn_s, mm_s, sb_rows_p, sb_rows_s)
```

```python
import functools

import numpy as np
import jax
import jax.numpy as jnp
from jax import lax
from jax.experimental import pallas as pl
from jax.experimental.pallas import tpu as pltpu

D_MODEL = 1024
HEAD_DIM = 64
NSA_HEADS = 8
NSA_KV_HEADS = 2
NSA_GROUP = NSA_HEADS // NSA_KV_HEADS
CMP_BLOCK = 32
SEL_BLOCK = 64
TOP_N = 16
WINDOW = 512
SEL_QUERY_BLOCK = 64
FORCED_SCORE = 1e4
MLSTM_HEADS = 4
MLSTM_DIM = 128
MLSTM_CHUNK = 64
SB_HEADS = D_MODEL // HEAD_DIM
QUERY_BLOCK = 128
N_EXPERTS = 8
TOP_K = 2
EPS = 1e-6
NEG_BIG = -1e30
TINY = 1e-30

IN0_SIZES = (NSA_HEADS * HEAD_DIM,) + (NSA_KV_HEADS * HEAD_DIM,) * 6 + (3 * NSA_HEADS,) + (MLSTM_HEADS * MLSTM_DIM,) * 4 + (MLSTM_HEADS, MLSTM_HEADS)
IN0_COLS = sum(IN0_SIZES)

LANES = 128
TOKEN_TILE = 512
VMEM_LIMIT = 48 * 1024 * 1024
BF16 = jnp.bfloat16
F32 = jnp.float32


def _cparams(sem):
    return pltpu.CompilerParams(dimension_semantics=sem, vmem_limit_bytes=VMEM_LIMIT)


def _rmsnorm_kernel(x_ref, g_ref, o_ref):
    x = x_ref[...]
    y = x * lax.rsqrt(jnp.mean(x * x, axis=-1, keepdims=True) + EPS)
    o_ref[...] = (y * g_ref[...]).astype(o_ref.dtype)


def rmsnorm_tokens(x, g, out_dtype):
    n, d = x.shape
    return pl.pallas_call(
        _rmsnorm_kernel,
        grid=(n // TOKEN_TILE,),
        in_specs=[pl.BlockSpec((TOKEN_TILE, d), lambda i: (i, 0)),
                  pl.BlockSpec((1, d), lambda i: (0, 0))],
        out_specs=pl.BlockSpec((TOKEN_TILE, d), lambda i: (i, 0)),
        out_shape=jax.ShapeDtypeStruct((n, d), out_dtype),
        compiler_params=_cparams(("parallel",)),
        name="rmsnorm",
    )(x, g.reshape(1, d))


def _rmsnorm_router_kernel(x_ref, g_ref, wr_ref, o_ref, logit_ref):
    x = x_ref[...]
    y = x * lax.rsqrt(jnp.mean(x * x, axis=-1, keepdims=True) + EPS) * g_ref[...]
    o_ref[...] = y.astype(o_ref.dtype)
    logit_ref[...] = jnp.dot(y, wr_ref[...], precision=lax.Precision.HIGHEST,
                             preferred_element_type=F32)


def rmsnorm_router(x, g, w_router):
    n, d = x.shape
    wr = jnp.pad(w_router, ((0, 0), (0, LANES - w_router.shape[1])))
    return pl.pallas_call(
        _rmsnorm_router_kernel,
        grid=(n // TOKEN_TILE,),
        in_specs=[pl.BlockSpec((TOKEN_TILE, d), lambda i: (i, 0)),
                  pl.BlockSpec((1, d), lambda i: (0, 0)),
                  pl.BlockSpec((d, LANES), lambda i: (0, 0))],
        out_specs=[pl.BlockSpec((TOKEN_TILE, d), lambda i: (i, 0)),
                   pl.BlockSpec((TOKEN_TILE, LANES), lambda i: (i, 0))],
        out_shape=[jax.ShapeDtypeStruct((n, d), BF16),
                   jax.ShapeDtypeStruct((n, LANES), F32)],
        compiler_params=_cparams(("parallel",)),
        name="rmsnorm_router",
    )(x, g.reshape(1, d), wr)


def _matmul_kernel(a_ref, w_ref, o_ref):
    o_ref[...] = jnp.dot(a_ref[...], w_ref[...], preferred_element_type=F32)


def _matmul_dual_kernel(a_ref, w_ref, o_ref, ob_ref):
    y = jnp.dot(a_ref[...], w_ref[...], preferred_element_type=F32)
    o_ref[...] = y
    ob_ref[...] = y.astype(BF16)


def _matmul_res_kernel(a_ref, w_ref, r_ref, o_ref):
    o_ref[...] = r_ref[...] + jnp.dot(a_ref[...], w_ref[...], preferred_element_type=F32)


def matmul(a, w, res=None, tn=1024, also_bf16=False):
    n, k = a.shape
    m = w.shape[1]
    tn = min(tn, m)
    assert n % TOKEN_TILE == 0 and m % tn == 0
    assert not (also_bf16 and res is not None)
    in_specs = [pl.BlockSpec((TOKEN_TILE, k), lambda i, j: (i, 0)),
                pl.BlockSpec((k, tn), lambda i, j: (0, j))]
    args = [a, w]
    body = _matmul_kernel
    out_spec = pl.BlockSpec((TOKEN_TILE, tn), lambda i, j: (i, j))
    out_specs = out_spec
    out_shape = jax.ShapeDtypeStruct((n, m), F32)
    if res is not None:
        in_specs.append(pl.BlockSpec((TOKEN_TILE, tn), lambda i, j: (i, j)))
        args.append(res)
        body = _matmul_res_kernel
    if also_bf16:
        body = _matmul_dual_kernel
        out_specs = [out_spec, out_spec]
        out_shape = [out_shape, jax.ShapeDtypeStruct((n, m), BF16)]
    return pl.pallas_call(
        body,
        grid=(n // TOKEN_TILE, m // tn),
        in_specs=in_specs,
        out_specs=out_specs,
        out_shape=out_shape,
        compiler_params=_cparams(("parallel", "parallel")),
        name="matmul",
    )(*args)


def _swiglu_kernel(te_ref, nv_ref, x_ref, wg_ref, wu_ref, wd_ref, cw_ref, *rest, has_res):
    if has_res:
        r_ref, o_ref, acc_ref = rest
    else:
        o_ref, acc_ref = rest
    i = pl.program_id(0)
    j = pl.program_id(1)

    @pl.when(j == 0)
    def _():
        acc_ref[...] = jnp.zeros_like(acc_ref)

    @pl.when(i < nv_ref[0])
    def _():
        x = x_ref[...]
        g = jnp.dot(x, wg_ref[0], preferred_element_type=F32)
        u = jnp.dot(x, wu_ref[0], preferred_element_type=F32)
        h = (g * jax.nn.sigmoid(g)) * u
        acc_ref[...] += jnp.dot(h.astype(BF16), wd_ref[0], preferred_element_type=F32)

    @pl.when(j == pl.num_programs(1) - 1)
    def _():
        y = acc_ref[...] * cw_ref[...]
        if has_res:
            y = y + r_ref[...]
        o_ref[...] = y


def grouped_swiglu(x, wg, wu, wd, tile_expert, n_valid, cw, res, tf):
    r, d = x.shape
    f = wg.shape[2]
    assert r % TOKEN_TILE == 0 and f % tf == 0
    has_res = res is not None
    in_specs = [
        pl.BlockSpec((TOKEN_TILE, d), lambda i, j, te, nv: (i, 0)),
        pl.BlockSpec((1, d, tf), lambda i, j, te, nv: (te[i], 0, j)),
        pl.BlockSpec((1, d, tf), lambda i, j, te, nv: (te[i], 0, j)),
        pl.BlockSpec((1, tf, d), lambda i, j, te, nv: (te[i], j, 0)),
        pl.BlockSpec((TOKEN_TILE, 1), lambda i, j, te, nv: (i, 0)),
    ]
    args = [x, wg, wu, wd, cw]
    if has_res:
        in_specs.append(pl.BlockSpec((TOKEN_TILE, d), lambda i, j, te, nv: (i, 0)))
        args.append(res)
    grid_spec = pltpu.PrefetchScalarGridSpec(
        num_scalar_prefetch=2,
        grid=(r // TOKEN_TILE, f // tf),
        in_specs=in_specs,
        out_specs=pl.BlockSpec((TOKEN_TILE, d), lambda i, j, te, nv: (i, 0)),
        scratch_shapes=[pltpu.VMEM((TOKEN_TILE, d), F32)],
    )
    return pl.pallas_call(
        functools.partial(_swiglu_kernel, has_res=has_res),
        grid_spec=grid_spec,
        out_shape=jax.ShapeDtypeStruct((r, d), F32),
        compiler_params=_cparams(("parallel", "arbitrary")),
        name="grouped_swiglu",
    )(tile_expert, n_valid, *args)


def dense_ffn(xn, res, wg, wu, wd):
    n = xn.shape[0]
    tiles = n // TOKEN_TILE
    return grouped_swiglu(xn, wg[None].astype(BF16), wu[None].astype(BF16), wd[None].astype(BF16),
                          jnp.zeros((tiles,), jnp.int32), jnp.full((1,), tiles, jnp.int32),
                          jnp.ones((n, 1), F32), res, tf=256)


def moe_ffn(xn, logits, res, wg, wu, wd):
    n, d = xn.shape
    probs = jax.nn.softmax(logits, axis=-1)
    top_w, top_i = lax.top_k(probs, TOP_K)
    top_w = top_w / jnp.sum(top_w, axis=-1, keepdims=True)
    flat_e = top_i.reshape(-1)
    order = jnp.argsort(flat_e, stable=True)
    sorted_e = flat_e[order]
    counts = jnp.sum(jax.nn.one_hot(flat_e, N_EXPERTS, dtype=jnp.int32), axis=0)
    padded = ((counts + TOKEN_TILE - 1) // TOKEN_TILE) * TOKEN_TILE
    ends_p = jnp.cumsum(padded)
    starts_p = ends_p - padded
    starts = jnp.cumsum(counts) - counts
    s_ix = jnp.arange(TOP_K * n, dtype=jnp.int32)
    dest = starts_p[sorted_e] + (s_ix - starts[sorted_e])
    rows = TOP_K * n + N_EXPERTS * TOKEN_TILE
    src_tok = jnp.zeros((rows,), jnp.int32).at[dest].set(order // TOP_K)
    cw = jnp.zeros((rows,), F32).at[dest].set(top_w.reshape(-1)[order])
    pos = jnp.zeros((TOP_K * n,), jnp.int32).at[order].set(dest)
    tiles = rows // TOKEN_TILE
    tile_start = jnp.arange(tiles, dtype=jnp.int32) * TOKEN_TILE
    tile_expert = jnp.minimum(jnp.searchsorted(ends_p, tile_start, side="right"), N_EXPERTS - 1).astype(jnp.int32)
    n_valid = (ends_p[-1] // TOKEN_TILE).astype(jnp.int32).reshape(1)
    x_sorted = xn[src_tok]
    y_sorted = grouped_swiglu(x_sorted, wg.astype(BF16), wu.astype(BF16), wd.astype(BF16),
                              tile_expert, n_valid, cw.reshape(rows, 1), None, tf=512)
    pos = pos.reshape(n, TOP_K)
    return res + y_sorted[pos[:, 0]] + y_sorted[pos[:, 1]]


SB_QT = 256
SB_KT = 128


def _sb_prompt_kernel(q_ref, k_ref, v_ref, u_ref, o_ref, acc_ref, r_ref):
    qi = pl.program_id(2)
    half = LANES // 2
    lane = lax.broadcasted_iota(jnp.int32, (SB_QT, LANES), 1)
    q = q_ref[...] * jnp.asarray(HEAD_DIM ** -0.5, BF16)
    zero = jnp.zeros_like(q)
    q_heads = (jnp.where(lane < half, q, zero), jnp.where(lane >= half, q, zero))
    u = u_ref[...]
    acc_ref[...] = jnp.zeros_like(acc_ref)
    r_ref[...] = jnp.zeros_like(r_ref)
    row = lax.broadcasted_iota(jnp.int32, (SB_QT, SB_KT), 0)
    col = lax.broadcasted_iota(jnp.int32, (SB_QT, SB_KT), 1)

    def tile(kt, masked):
        start = pl.multiple_of(kt * SB_KT, SB_KT)
        k = k_ref[pl.ds(start, SB_KT), :]
        v = v_ref[pl.ds(start, SB_KT), :]
        if masked:
            before = (kt * SB_KT + col) < (qi * SB_QT + row)
        pv = []
        for hh in range(2):
            z = lax.dot_general(q_heads[hh], k, (((1,), (1,)), ((), ())), preferred_element_type=F32)
            l = jnp.log1p(jnp.exp(-jnp.abs(z)))
            log_keep = -(jnp.maximum(z, 0.0) + l)
            log_beta = jnp.minimum(z, 0.0) - l
            if masked:
                log_keep = jnp.where(before, log_keep, 0.0)
            hi = log_keep.astype(BF16)
            lo = (log_keep - hi.astype(F32)).astype(BF16)
            cs = jnp.dot(hi, u, preferred_element_type=F32) + jnp.dot(lo, u, preferred_element_type=F32)
            r_prev = r_ref[hh]
            a = jnp.exp(log_beta + cs[:, :SB_KT] + r_prev)
            if masked:
                a = jnp.where(before, a, 0.0)
            r_ref[hh] = r_prev + cs[:, SB_KT:]
            pv.append(jnp.dot(a.astype(BF16), v, preferred_element_type=F32))
        acc_ref[...] += jnp.where(lane < half, pv[0], pv[1])

    n_diag = SB_QT // SB_KT
    last = qi * n_diag + n_diag - 1
    for d in range(n_diag):
        tile(last - d, True)

    def body(i, carry):
        tile(qi * n_diag - 1 - i, False)
        return carry

    lax.fori_loop(0, qi * n_diag, body, 0)
    o_ref[...] = acc_ref[...].astype(o_ref.dtype)


def sb_prompt_attention(qkv, batch, seq):
    d = SB_HEADS * HEAD_DIM
    n_pair = d // LANES
    nq = seq // SB_QT
    jj = np.arange(SB_KT)
    u = np.concatenate([(jj[:, None] > jj[None, :]).astype(np.float32), np.ones((SB_KT, SB_KT), np.float32)], axis=1)
    return pl.pallas_call(
        _sb_prompt_kernel,
        grid=(batch, n_pair, nq),
        in_specs=[pl.BlockSpec((SB_QT, LANES), lambda b, p, i: (b * nq + i, p)),
                  pl.BlockSpec((seq, LANES), lambda b, p, i: (b, n_pair + p)),
                  pl.BlockSpec((seq, LANES), lambda b, p, i: (b, 2 * n_pair + p)),
                  pl.BlockSpec((SB_KT, 2 * SB_KT), lambda b, p, i: (0, 0))],
        out_specs=pl.BlockSpec((SB_QT, LANES), lambda b, p, i: (b * nq + i, p)),
        out_shape=jax.ShapeDtypeStruct((batch * seq, d), BF16),
        scratch_shapes=[pltpu.VMEM((SB_QT, LANES), F32), pltpu.VMEM((2, SB_QT, SB_KT), F32)],
        compiler_params=_cparams(("parallel", "parallel", "arbitrary")),
        name="sb_prompt_attention",
    )(qkv, qkv, qkv, jnp.asarray(u, BF16))


NSA_QT = 128
NSA_KT = 512


def _nsa_sel_kernel(q_ref, k_ref, v_ref, sel_ref, e_ref, o_ref, m_ref, l_ref, acc_ref):
    qi = pl.program_id(2)
    kt = pl.program_id(3)
    width = NSA_GROUP * HEAD_DIM
    lane_head = lax.broadcasted_iota(jnp.int32, (NSA_QT, width), 1) // HEAD_DIM

    @pl.when(kt == 0)
    def _():
        m_ref[...] = jnp.full_like(m_ref, NEG_BIG)
        l_ref[...] = jnp.zeros_like(l_ref)
        acc_ref[...] = jnp.zeros_like(acc_ref)

    @pl.when(kt * NSA_KT <= qi * NSA_QT + NSA_QT - 1)
    def _():
        q = q_ref[...] * jnp.asarray(HEAD_DIM ** -0.5, BF16)
        zero = jnp.zeros_like(q)
        sel = sel_ref[...]
        row = lax.broadcasted_iota(jnp.int32, (NSA_QT, LANES), 0)
        col = lax.broadcasted_iota(jnp.int32, (NSA_QT, LANES), 1)
        for j in range(NSA_KT // LANES):
            k = k_ref[j * LANES:(j + 1) * LANES, :]
            v = v_ref[j * LANES:(j + 1) * LANES, :]
            chosen = jnp.dot(sel, e_ref[:, j * LANES:(j + 1) * LANES], preferred_element_type=F32)
            valid = jnp.logical_and(chosen > 0.5, (kt * NSA_KT + j * LANES + col) <= (qi * NSA_QT + row))
            for r in range(NSA_GROUP):
                qr = jnp.where(lane_head == r, q, zero)
                s = lax.dot_general(qr, k, (((1,), (1,)), ((), ())), preferred_element_type=F32)
                s = jnp.where(valid, s, NEG_BIG)
                m_prev = m_ref[r]
                m_new = jnp.maximum(m_prev, jnp.max(s, axis=-1, keepdims=True))
                alpha = jnp.exp(m_prev - m_new)
                p = jnp.where(valid, jnp.exp(s - m_new), 0.0)
                l_ref[r] = alpha * l_ref[r] + jnp.sum(p, axis=-1, keepdims=True)
                m_ref[r] = m_new
                pv = jnp.dot(p.astype(BF16), v, preferred_element_type=F32)
                acc_ref[r] = jnp.concatenate([alpha, alpha], axis=1) * acc_ref[r] + pv

    @pl.when(kt == pl.num_programs(3) - 1)
    def _():
        out = jnp.zeros((NSA_QT, width), F32)
        for r in range(NSA_GROUP):
            denom = jnp.maximum(l_ref[r], TINY)
            o_r = acc_ref[r] / jnp.concatenate([denom, denom], axis=1)
            out = jnp.where(lane_head == r, o_r, out)
        o_ref[...] = out


def nsa_selected_prompt(q, k_rep, v_rep, sel, batch, seq):
    width = NSA_GROUP * HEAD_DIM
    n_sel = seq // SEL_BLOCK
    nq = seq // NSA_QT
    nk = seq // NSA_KT
    expand = (np.arange(n_sel)[:, None] == (np.arange(seq)[None, :] // SEL_BLOCK)).astype(np.float32)

    def kv_map(b, g, i, t):
        return (b * nk + jnp.minimum(t, (i * NSA_QT + NSA_QT - 1) // NSA_KT), g)

    return pl.pallas_call(
        _nsa_sel_kernel,
        grid=(batch, NSA_KV_HEADS, nq, nk),
        in_specs=[pl.BlockSpec((NSA_QT, width), lambda b, g, i, t: (b * nq + i, g)),
                  pl.BlockSpec((NSA_KT, width), kv_map),
                  pl.BlockSpec((NSA_KT, width), kv_map),
                  pl.BlockSpec((NSA_QT, n_sel), lambda b, g, i, t: ((b * NSA_KV_HEADS + g) * nq + i, 0)),
                  pl.BlockSpec((n_sel, NSA_KT), lambda b, g, i, t: (0, jnp.minimum(t, (i * NSA_QT + NSA_QT - 1) // NSA_KT)))],
        out_specs=pl.BlockSpec((NSA_QT, width), lambda b, g, i, t: (b * nq + i, g)),
        out_shape=jax.ShapeDtypeStruct((batch * seq, NSA_HEADS * HEAD_DIM), F32),
        scratch_shapes=[pltpu.VMEM((NSA_GROUP, NSA_QT, LANES), F32),
                        pltpu.VMEM((NSA_GROUP, NSA_QT, LANES), F32),
                        pltpu.VMEM((NSA_GROUP, NSA_QT, width), F32)],
        compiler_params=_cparams(("parallel", "parallel", "parallel", "arbitrary")),
        name="nsa_selected_prompt",
    )(q, k_rep, v_rep, sel, jnp.asarray(expand, BF16))


def head_rmsnorm(h, g, dtype):
    y = h * lax.rsqrt(jnp.mean(h * h, axis=-1, keepdims=True) + EPS)
    return (y * g.astype(jnp.float32)).astype(dtype)


def masked_softmax(s, valid, axes):
    s = jnp.where(valid, s, NEG_BIG)
    m = jnp.max(s, axis=axes, keepdims=True)
    e = jnp.where(valid, jnp.exp(s - m), 0.0)
    return e / jnp.maximum(jnp.sum(e, axis=axes, keepdims=True), TINY)


def split_cols(h, sizes):
    offs = [int(o) for o in np.cumsum(sizes)[:-1]]
    return jnp.split(h, offs, axis=-1)


def nsa_compress(rows, pos_emb, w):
    B, Tk, G, dh = rows.shape
    blk = rows.reshape(B, Tk // CMP_BLOCK, CMP_BLOCK, G, dh) + pos_emb[None, None, :, None, :]
    blk = blk.transpose(0, 1, 3, 2, 4).reshape(B, Tk // CMP_BLOCK, G, CMP_BLOCK * dh)
    return blk @ w


def nsa_compressed_selected(q, rows, q_pos, cmp_pos, cmp_w):
    B, Tq, H, dh = q.shape
    G, R = NSA_KV_HEADS, NSA_GROUP
    pad = (-rows.shape[1]) % SEL_BLOCK
    rows = jnp.pad(rows, ((0, 0), (0, pad), (0, 0), (0, 0), (0, 0)))
    Tp = rows.shape[1]
    scale = dh ** -0.5
    qg = q.reshape(B, Tq, G, R, dh)
    k_cmp = nsa_compress(rows[:, :, 0], cmp_pos[0], cmp_w[0])
    v_cmp = nsa_compress(rows[:, :, 1], cmp_pos[1], cmp_w[1])
    n_cmp = Tp // CMP_BLOCK
    s = jnp.einsum('btgrd,bngd->bgrtn', qg, k_cmp).astype(jnp.float32) * scale
    cmp_end = (jnp.arange(n_cmp) + 1) * CMP_BLOCK - 1
    p = masked_softmax(s, cmp_end[None, :] <= q_pos[:, None], -1)
    o_cmp = jnp.einsum('bgrtn,bngd->btgrd', p.astype(v_cmp.dtype), v_cmp).reshape(B, Tq, H, dh)
    n_sel = Tp // SEL_BLOCK
    imp = p.sum(axis=2).reshape(B, G, Tq, n_sel, SEL_BLOCK // CMP_BLOCK).sum(axis=-1)
    blk = jnp.arange(n_sel)
    forced = (blk[None, :] == 0) | (blk[None, :] == (q_pos // SEL_BLOCK)[:, None])
    avail = blk[None, :] * SEL_BLOCK <= q_pos[:, None]
    imp = jnp.where(avail, jnp.where(forced, FORCED_SCORE, imp), -1.0)
    _, idx = lax.top_k(imp, min(TOP_N, n_sel))
    n_top = idx.shape[-1]
    if Tq == Tp and Tq % NSA_KT == 0:
        sel = jnp.sum(jax.nn.one_hot(idx, n_sel, dtype=BF16), axis=-2).reshape(B * G * Tq, n_sel)
        rep = lambda a: jnp.tile(a.astype(BF16)[:, :, :, None, :], (1, 1, 1, R, 1)).reshape(B * Tq, G * R * dh)
        o_sel = nsa_selected_prompt(q.astype(BF16).reshape(B * Tq, H * dh), rep(rows[:, :, 2]), rep(rows[:, :, 3]),
                                    sel, B, Tq)
        return o_cmp, o_sel.reshape(B, Tq, H, dh)
    ks = rows[:, :, 2].reshape(B, n_sel, SEL_BLOCK, G, dh).transpose(0, 3, 1, 2, 4)
    vs = rows[:, :, 3].reshape(B, n_sel, SEL_BLOCK, G, dh).transpose(0, 3, 1, 2, 4)
    qb = SEL_QUERY_BLOCK if Tq % SEL_QUERY_BLOCK == 0 else Tq
    nqb = Tq // qb
    q_blocks = qg.reshape(B, nqb, qb, G, R, dh).swapaxes(0, 1)
    idx_blocks = idx.reshape(B, G, nqb, qb, n_top).transpose(2, 0, 1, 3, 4)
    pos_blocks = q_pos.reshape(nqb, qb)
    b_ix = jnp.arange(B)[:, None, None, None]
    g_ix = jnp.arange(G)[None, :, None, None]

    def sel_block(args):
        qc, ic, pc = args
        kg = ks[b_ix, g_ix, ic]
        vg = vs[b_ix, g_ix, ic]
        sc = jnp.einsum('bqgrd,bgqnkd->bgrqnk', qc, kg).astype(jnp.float32) * scale
        kpos = ic[..., None] * SEL_BLOCK + jnp.arange(SEL_BLOCK)
        ok = (kpos <= pc[None, None, :, None, None])[:, :, None]
        w = masked_softmax(sc, ok, (-2, -1))
        return jnp.einsum('bgrqnk,bgqnkd->bqgrd', w.astype(vg.dtype), vg)

    o_sel = lax.map(sel_block, (q_blocks, idx_blocks, pos_blocks))
    o_sel = o_sel.swapaxes(0, 1).reshape(B, Tq, H, dh)
    return o_cmp, o_sel


def window_core(qg, k, v, q_pos, k_pos):
    s = jnp.einsum('btgrd,bkgd->bgrtk', qg, k).astype(jnp.float32) * (qg.shape[-1] ** -0.5)
    dist = q_pos[:, None] - k_pos[None, :]
    ok = (dist >= 0) & (dist < WINDOW) & (k_pos[None, :] >= 0)
    w = masked_softmax(s, ok, -1)
    return jnp.einsum('bgrtk,bkgd->btgrd', w.astype(v.dtype), v)


def window_banded(qg, k, v):
    B, T = qg.shape[:2]
    qb = QUERY_BLOCK if T % QUERY_BLOCK == 0 else T
    nqb = T // qb
    span = WINDOW + qb
    kidx = jnp.arange(nqb)[:, None] * qb + jnp.arange(span)[None, :]
    pad = ((0, 0), (WINDOW, 0), (0, 0), (0, 0))
    kb = jnp.pad(k, pad)[:, kidx]
    vb = jnp.pad(v, pad)[:, kidx]
    q_pos = jnp.arange(T).reshape(nqb, qb)
    k_pos = kidx - WINDOW
    q_blocks = qg.reshape(B, nqb, qb, *qg.shape[2:])
    o = jax.vmap(window_core, in_axes=(1, 1, 1, 0, 0), out_axes=1)(q_blocks, kb, vb, q_pos, k_pos)
    return o.reshape(qg.shape)


def mlstm_chunkwise(q, k, v, i_pre, f_pre, C0, n0, m0):
    B, T, H, d = q.shape
    L = MLSTM_CHUNK if T % MLSTM_CHUNK == 0 else T
    nc = T // L
    f32 = jnp.float32
    q, k, v = q.astype(f32), k.astype(f32) * (d ** -0.5), v.astype(f32)
    log_f = jax.nn.log_sigmoid(f_pre.astype(f32))
    i_pre = i_pre.astype(f32)
    chunks = lambda a: a.reshape(B, nc, L, *a.shape[2:]).swapaxes(0, 1)
    causal = jnp.tril(jnp.ones((L, L), dtype=bool))

    def step(carry, xs):
        C, n, m = carry
        qc, kc, vc, ic, fc = xs
        b = jnp.cumsum(fc, axis=1).swapaxes(1, 2)
        it = ic.swapaxes(1, 2)
        log_d = jnp.where(causal, b[..., :, None] - b[..., None, :] + it[..., None, :], -jnp.inf)
        m_inter = b + m[..., None]
        m_t = jnp.maximum(m_inter, jnp.max(log_d, axis=-1))
        w = jnp.einsum('blhd,bshd->bhls', qc, kc) * jnp.exp(log_d - m_t[..., None])
        carry_w = jnp.exp(m_inter - m_t)
        num = jnp.einsum('bhls,bshd->blhd', w, vc) + jnp.einsum('blhd,bhde->blhe', qc, C) * carry_w.swapaxes(1, 2)[..., None]
        den = jnp.sum(w, axis=-1) + jnp.einsum('blhd,bhd->bhl', qc, n) * carry_w
        den = jnp.maximum(jnp.abs(den), jnp.exp(-m_t))
        h = num / den.swapaxes(1, 2)[..., None]
        m_new = m_t[..., -1]
        w_end = jnp.exp(b[..., -1:] - b + it - m_new[..., None])
        decay = jnp.exp(b[..., -1] + m - m_new)
        C_new = decay[..., None, None] * C + jnp.einsum('bhs,bshd,bshe->bhde', w_end, kc, vc)
        n_new = decay[..., None] * n + jnp.einsum('bhs,bshd->bhd', w_end, kc)
        return (C_new, n_new, m_new), h

    (C, n, m), hs = lax.scan(step, (C0.astype(f32), n0.astype(f32), m0.astype(f32)),
                             tuple(chunks(a) for a in (q, k, v, i_pre, log_f)))
    return hs.swapaxes(0, 1).reshape(B, T, H, d), C, n, m


def nsa_mlstm_core(proj, start, past_rows, win_buf, C0, n0, m0, nsa_cmp_pos, nsa_cmp_w,
                   mlstm_b_i, mlstm_b_f, mlstm_norm):
    B, T, _ = proj.shape
    (q_a, k_c, v_c, k_s, v_s, k_w, v_w, g_a, q_m, k_m, v_m, o_m, i_m, f_m) = split_cols(proj, IN0_SIZES)
    kvh = lambda a: a.reshape(B, T, NSA_KV_HEADS, HEAD_DIM)
    mh = lambda a: a.reshape(B, T, MLSTM_HEADS, MLSTM_DIM)
    q_a = q_a.reshape(B, T, NSA_HEADS, HEAD_DIM)
    q_pos = start + jnp.arange(T)
    new_rows = jnp.stack([kvh(k_c), kvh(v_c), kvh(k_s), kvh(v_s)], axis=2)
    rows_full = jnp.concatenate([past_rows.astype(new_rows.dtype), new_rows], axis=1)
    o_cmp, o_sel = nsa_compressed_selected(q_a, rows_full, q_pos, nsa_cmp_pos, nsa_cmp_w)
    win_rows = jnp.stack([kvh(k_w), kvh(v_w)], axis=2)
    qg = q_a.reshape(B, T, NSA_KV_HEADS, NSA_GROUP, HEAD_DIM)
    if win_buf is None:
        o_win = window_banded(qg, win_rows[:, :, 0], win_rows[:, :, 1])
        new_win = win_rows[:, T - min(WINDOW, T):]
    else:
        n_buf = win_buf.shape[1]
        buf = jnp.concatenate([win_buf.astype(win_rows.dtype), win_rows], axis=1)
        k_pos = start - n_buf + jnp.arange(n_buf + T)
        o_win = window_core(qg, buf[:, :, 0], buf[:, :, 1], q_pos, k_pos)
        new_win = buf[:, T:]
    o_win = o_win.reshape(B, T, NSA_HEADS, HEAD_DIM)
    gates = jax.nn.sigmoid(g_a.reshape(B, T, NSA_HEADS, 3))
    o_a = gates[..., 0:1] * o_cmp + gates[..., 1:2] * o_sel + gates[..., 2:3] * o_win
    h_m, C, n, m = mlstm_chunkwise(mh(q_m), mh(k_m), mh(v_m), i_m + mlstm_b_i, f_m + mlstm_b_f, C0, n0, m0)
    h_m = head_rmsnorm(h_m, mlstm_norm, proj.dtype) * jax.nn.sigmoid(mh(o_m))
    mixed = jnp.concatenate([o_a.reshape(B, T, -1), h_m.reshape(B, T, -1)], axis=-1)
    return mixed, new_rows, new_win, C, n, m


def sb_attend(q, k, v, start):
    B, Tq, H, dh = q.shape
    Tk = k.shape[1]
    qb = QUERY_BLOCK if Tq % QUERY_BLOCK == 0 else Tq
    nqb = Tq // qb
    q_blocks = q.reshape(B, nqb, qb, H, dh).swapaxes(0, 1)
    pos_blocks = (start + jnp.arange(Tq)).reshape(nqb, qb)
    k_pos = jnp.arange(Tk)
    scale = dh ** -0.5

    def block(args):
        qc, pc = args
        z = jnp.einsum('bqhd,bkhd->bhqk', qc, k).astype(jnp.float32) * scale
        before = k_pos[None, :] < pc[:, None]
        log_keep = jnp.where(before, -jax.nn.softplus(z), 0.0)
        tail = lax.cumsum(log_keep, axis=3, reverse=True) - log_keep
        a = jnp.where(before, jnp.exp(jax.nn.log_sigmoid(z) + tail), 0.0)
        return jnp.einsum('bhqk,bkhd->bqhd', a.astype(v.dtype), v)

    o = lax.map(block, (q_blocks, pos_blocks))
    return o.swapaxes(0, 1).reshape(B, Tq, H, dh)


def sb_core(qkv, start, past_k, past_v):
    k_full = jnp.concatenate([past_k.astype(qkv.dtype), qkv[:, :, 1]], axis=1)
    v_full = jnp.concatenate([past_v.astype(qkv.dtype), qkv[:, :, 2]], axis=1)
    o = sb_attend(qkv[:, :, 0], k_full, v_full, start)
    return o.reshape(o.shape[0], o.shape[1], -1), qkv[:, :, 1:]


def kernel(x_prompt, x_sample, cache_nsa_kv, state_nsa_win, state_mlstm_C, state_mlstm_n, state_mlstm_m, cache_sb_kv, page_table, norm_mix0, w_in0, nsa_cmp_pos, nsa_cmp_w, mlstm_b_i, mlstm_b_f, mlstm_norm, w_out0, norm_ffn0, w_gate0, w_up0, w_down0, norm_mix1, w_qkv1, w_out1, norm_ffn1, w_router1, w_gate1, w_up1, w_down1, norm_final):
    B, T, D = x_prompt.shape
    DB, TS, _ = x_sample.shape
    n_p = B * T
    past_len = page_table.shape[1] * cache_nsa_kv.shape[1]
    x = jnp.concatenate([x_prompt.reshape(n_p, D), x_sample.reshape(DB * TS, D)], axis=0)

    hn = rmsnorm_tokens(x, norm_mix0, BF16)
    in0_pad = (-IN0_COLS) % (9 * LANES)
    w_in = jnp.pad(w_in0, ((0, 0), (0, in0_pad))).astype(BF16)
    proj = matmul(hn, w_in, tn=(IN0_COLS + in0_pad) // 3)[:, :IN0_COLS]
    proj_p = proj[:n_p].reshape(B, T, IN0_COLS)
    proj_s = proj[n_p:].reshape(DB, TS, IN0_COLS)
    empty = jnp.zeros((B, 0, 4, NSA_KV_HEADS, HEAD_DIM), F32)
    c0 = jnp.zeros((B, MLSTM_HEADS, MLSTM_DIM, MLSTM_DIM), F32)
    n0 = jnp.zeros((B, MLSTM_HEADS, MLSTM_DIM), F32)
    m0 = jnp.zeros((B, MLSTM_HEADS), F32)
    mixed_p, nsa_rows_p, nsa_win_p, mC_p, mn_p, mm_p = nsa_mlstm_core(
        proj_p, 0, empty, None, c0, n0, m0, nsa_cmp_pos, nsa_cmp_w, mlstm_b_i, mlstm_b_f, mlstm_norm)
    past_rows = cache_nsa_kv[page_table].reshape(DB, past_len, 4, NSA_KV_HEADS, HEAD_DIM)
    mixed_s, nsa_rows_s, nsa_win_s, mC_s, mn_s, mm_s = nsa_mlstm_core(
        proj_s, past_len, past_rows, state_nsa_win, state_mlstm_C, state_mlstm_n, state_mlstm_m,
        nsa_cmp_pos, nsa_cmp_w, mlstm_b_i, mlstm_b_f, mlstm_norm)
    mixed = jnp.concatenate([mixed_p.reshape(n_p, D), mixed_s.reshape(DB * TS, D)], axis=0)
    x = matmul(mixed.astype(BF16), w_out0.astype(BF16), res=x)
    x = dense_ffn(rmsnorm_tokens(x, norm_ffn0, BF16), x, w_gate0, w_up0, w_down0)

    hn = rmsnorm_tokens(x, norm_mix1, BF16)
    qkv, qkv_b = matmul(hn, w_qkv1.astype(BF16), also_bf16=True)
    sb_rows_p = qkv[:n_p].reshape(B, T, 3, SB_HEADS, HEAD_DIM)[:, :, 1:]
    qkv_s = qkv[n_p:].reshape(DB, TS, 3, SB_HEADS, HEAD_DIM)
    o_p = sb_prompt_attention(qkv_b, B, T)
    past_k = cache_sb_kv[page_table, :, 0].reshape(DB, past_len, SB_HEADS, HEAD_DIM)
    past_v = cache_sb_kv[page_table, :, 1].reshape(DB, past_len, SB_HEADS, HEAD_DIM)
    o_s, sb_rows_s = sb_core(qkv_s, past_len, past_k, past_v)
    o = jnp.concatenate([o_p, o_s.reshape(DB * TS, D).astype(BF16)], axis=0)
    x = matmul(o, w_out1.astype(BF16), res=x)
    hn, logits = rmsnorm_router(x, norm_ffn1, w_router1)
    x = moe_ffn(hn, logits[:, :N_EXPERTS], x, w_gate1, w_up1, w_down1)

    y = rmsnorm_tokens(x, norm_final, F32)
    y_prompt = y[:n_p].reshape(B, T, D)
    y_sample = y[n_p:].reshape(DB, TS, D)
    return (y_prompt, y_sample, nsa_rows_p, nsa_rows_s, nsa_win_p, nsa_win_s, mC_p, mn_p, mm_p,
            mC_s, mn_s, mm_s, sb_rows_p, sb_rows_s)
```

```python
import functools

import numpy as np
import jax
import jax.numpy as jnp
from jax import lax
from jax.experimental import pallas as pl
from jax.experimental.pallas import tpu as pltpu

D_MODEL = 1024
HEAD_DIM = 64
NSA_HEADS = 8
NSA_KV_HEADS = 2
NSA_GROUP = NSA_HEADS // NSA_KV_HEADS
CMP_BLOCK = 32
SEL_BLOCK = 64
TOP_N = 16
WINDOW = 512
SEL_QUERY_BLOCK = 64
FORCED_SCORE = 1e4
MLSTM_HEADS = 4
MLSTM_DIM = 128
MLSTM_CHUNK = 64
SB_HEADS = D_MODEL // HEAD_DIM
QUERY_BLOCK = 128
N_EXPERTS = 8
TOP_K = 2
EPS = 1e-6
NEG_BIG = -1e30
TINY = 1e-30

IN0_SIZES = (NSA_HEADS * HEAD_DIM,) + (NSA_KV_HEADS * HEAD_DIM,) * 6 + (3 * NSA_HEADS,) + (MLSTM_HEADS * MLSTM_DIM,) * 4 + (MLSTM_HEADS, MLSTM_HEADS)
IN0_COLS = sum(IN0_SIZES)

LANES = 128
TOKEN_TILE = 512
VMEM_LIMIT = 48 * 1024 * 1024
BF16 = jnp.bfloat16
F32 = jnp.float32


def _cparams(sem):
    return pltpu.CompilerParams(dimension_semantics=sem, vmem_limit_bytes=VMEM_LIMIT)


def _rmsnorm_kernel(x_ref, g_ref, o_ref):
    x = x_ref[...]
    y = x * lax.rsqrt(jnp.mean(x * x, axis=-1, keepdims=True) + EPS)
    o_ref[...] = (y * g_ref[...]).astype(o_ref.dtype)


def rmsnorm_tokens(x, g, out_dtype):
    n, d = x.shape
    return pl.pallas_call(
        _rmsnorm_kernel,
        grid=(n // TOKEN_TILE,),
        in_specs=[pl.BlockSpec((TOKEN_TILE, d), lambda i: (i, 0)),
                  pl.BlockSpec((1, d), lambda i: (0, 0))],
        out_specs=pl.BlockSpec((TOKEN_TILE, d), lambda i: (i, 0)),
        out_shape=jax.ShapeDtypeStruct((n, d), out_dtype),
        compiler_params=_cparams(("parallel",)),
        name="rmsnorm",
    )(x, g.reshape(1, d))


def _rmsnorm_router_kernel(x_ref, g_ref, wr_ref, o_ref, logit_ref):
    x = x_ref[...]
    y = x * lax.rsqrt(jnp.mean(x * x, axis=-1, keepdims=True) + EPS) * g_ref[...]
    o_ref[...] = y.astype(o_ref.dtype)
    logit_ref[...] = jnp.dot(y, wr_ref[...], precision=lax.Precision.HIGHEST,
                             preferred_element_type=F32)


def rmsnorm_router(x, g, w_router):
    n, d = x.shape
    wr = jnp.pad(w_router, ((0, 0), (0, LANES - w_router.shape[1])))
    return pl.pallas_call(
        _rmsnorm_router_kernel,
        grid=(n // TOKEN_TILE,),
        in_specs=[pl.BlockSpec((TOKEN_TILE, d), lambda i: (i, 0)),
                  pl.BlockSpec((1, d), lambda i: (0, 0)),
                  pl.BlockSpec((d, LANES), lambda i: (0, 0))],
        out_specs=[pl.BlockSpec((TOKEN_TILE, d), lambda i: (i, 0)),
                   pl.BlockSpec((TOKEN_TILE, LANES), lambda i: (i, 0))],
        out_shape=[jax.ShapeDtypeStruct((n, d), BF16),
                   jax.ShapeDtypeStruct((n, LANES), F32)],
        compiler_params=_cparams(("parallel",)),
        name="rmsnorm_router",
    )(x, g.reshape(1, d), wr)


def _matmul_kernel(a_ref, w_ref, o_ref):
    o_ref[...] = jnp.dot(a_ref[...], w_ref[...], preferred_element_type=F32)


def _matmul_dual_kernel(a_ref, w_ref, o_ref, ob_ref):
    y = jnp.dot(a_ref[...], w_ref[...], preferred_element_type=F32)
    o_ref[...] = y
    ob_ref[...] = y.astype(BF16)


def _matmul_res_kernel(a_ref, w_ref, r_ref, o_ref):
    o_ref[...] = r_ref[...] + jnp.dot(a_ref[...], w_ref[...], preferred_element_type=F32)


def matmul(a, w, res=None, tn=1024, also_bf16=False):
    n, k = a.shape
    m = w.shape[1]
    tn = min(tn, m)
    assert n % TOKEN_TILE == 0 and m % tn == 0
    assert not (also_bf16 and res is not None)
    in_specs = [pl.BlockSpec((TOKEN_TILE, k), lambda i, j: (i, 0)),
                pl.BlockSpec((k, tn), lambda i, j: (0, j))]
    args = [a, w]
    body = _matmul_kernel
    out_spec = pl.BlockSpec((TOKEN_TILE, tn), lambda i, j: (i, j))
    out_specs = out_spec
    out_shape = jax.ShapeDtypeStruct((n, m), F32)
    if res is not None:
        in_specs.append(pl.BlockSpec((TOKEN_TILE, tn), lambda i, j: (i, j)))
        args.append(res)
        body = _matmul_res_kernel
    if also_bf16:
        body = _matmul_dual_kernel
        out_specs = [out_spec, out_spec]
        out_shape = [out_shape, jax.ShapeDtypeStruct((n, m), BF16)]
    return pl.pallas_call(
        body,
        grid=(n // TOKEN_TILE, m // tn),
        in_specs=in_specs,
        out_specs=out_specs,
        out_shape=out_shape,
        compiler_params=_cparams(("parallel", "parallel")),
        name="matmul",
    )(*args)


def _swiglu_kernel(te_ref, nv_ref, x_ref, wg_ref, wu_ref, wd_ref, cw_ref, *rest, has_res):
    if has_res:
        r_ref, o_ref, acc_ref = rest
    else:
        o_ref, acc_ref = rest
    i = pl.program_id(0)
    j = pl.program_id(1)

    @pl.when(j == 0)
    def _():
        acc_ref[...] = jnp.zeros_like(acc_ref)

    @pl.when(i < nv_ref[0])
    def _():
        x = x_ref[...]
        g = jnp.dot(x, wg_ref[0], preferred_element_type=F32)
        u = jnp.dot(x, wu_ref[0], preferred_element_type=F32)
        h = (g * jax.nn.sigmoid(g)) * u
        acc_ref[...] += jnp.dot(h.astype(BF16), wd_ref[0], preferred_element_type=F32)

    @pl.when(j == pl.num_programs(1) - 1)
    def _():
        y = acc_ref[...] * cw_ref[...]
        if has_res:
            y = y + r_ref[...]
        o_ref[...] = y


def grouped_swiglu(x, wg, wu, wd, tile_expert, n_valid, cw, res, tf):
    r, d = x.shape
    f = wg.shape[2]
    assert r % TOKEN_TILE == 0 and f % tf == 0
    has_res = res is not None
    in_specs = [
        pl.BlockSpec((TOKEN_TILE, d), lambda i, j, te, nv: (i, 0)),
        pl.BlockSpec((1, d, tf), lambda i, j, te, nv: (te[i], 0, j)),
        pl.BlockSpec((1, d, tf), lambda i, j, te, nv: (te[i], 0, j)),
        pl.BlockSpec((1, tf, d), lambda i, j, te, nv: (te[i], j, 0)),
        pl.BlockSpec((TOKEN_TILE, 1), lambda i, j, te, nv: (i, 0)),
    ]
    args = [x, wg, wu, wd, cw]
    if has_res:
        in_specs.append(pl.BlockSpec((TOKEN_TILE, d), lambda i, j, te, nv: (i, 0)))
        args.append(res)
    grid_spec = pltpu.PrefetchScalarGridSpec(
        num_scalar_prefetch=2,
        grid=(r // TOKEN_TILE, f // tf),
        in_specs=in_specs,
        out_specs=pl.BlockSpec((TOKEN_TILE, d), lambda i, j, te, nv: (i, 0)),
        scratch_shapes=[pltpu.VMEM((TOKEN_TILE, d), F32)],
    )
    return pl.pallas_call(
        functools.partial(_swiglu_kernel, has_res=has_res),
        grid_spec=grid_spec,
        out_shape=jax.ShapeDtypeStruct((r, d), F32),
        compiler_params=_cparams(("parallel", "arbitrary")),
        name="grouped_swiglu",
    )(tile_expert, n_valid, *args)


def dense_ffn(xn, res, wg, wu, wd):
    n = xn.shape[0]
    tiles = n // TOKEN_TILE
    return grouped_swiglu(xn, wg[None].astype(BF16), wu[None].astype(BF16), wd[None].astype(BF16),
                          jnp.zeros((tiles,), jnp.int32), jnp.full((1,), tiles, jnp.int32),
                          jnp.ones((n, 1), F32), res, tf=256)


def moe_ffn(xn, logits, res, wg, wu, wd):
    n, d = xn.shape
    probs = jax.nn.softmax(logits, axis=-1)
    top_w, top_i = lax.top_k(probs, TOP_K)
    top_w = top_w / jnp.sum(top_w, axis=-1, keepdims=True)
    flat_e = top_i.reshape(-1)
    order = jnp.argsort(flat_e, stable=True)
    sorted_e = flat_e[order]
    counts = jnp.sum(jax.nn.one_hot(flat_e, N_EXPERTS, dtype=jnp.int32), axis=0)
    padded = ((counts + TOKEN_TILE - 1) // TOKEN_TILE) * TOKEN_TILE
    ends_p = jnp.cumsum(padded)
    starts_p = ends_p - padded
    starts = jnp.cumsum(counts) - counts
    s_ix = jnp.arange(TOP_K * n, dtype=jnp.int32)
    dest = starts_p[sorted_e] + (s_ix - starts[sorted_e])
    rows = TOP_K * n + N_EXPERTS * TOKEN_TILE
    src_tok = jnp.zeros((rows,), jnp.int32).at[dest].set(order // TOP_K)
    cw = jnp.zeros((rows,), F32).at[dest].set(top_w.reshape(-1)[order])
    pos = jnp.zeros((TOP_K * n,), jnp.int32).at[order].set(dest)
    tiles = rows // TOKEN_TILE
    tile_start = jnp.arange(tiles, dtype=jnp.int32) * TOKEN_TILE
    tile_expert = jnp.minimum(jnp.searchsorted(ends_p, tile_start, side="right"), N_EXPERTS - 1).astype(jnp.int32)
    n_valid = (ends_p[-1] // TOKEN_TILE).astype(jnp.int32).reshape(1)
    x_sorted = xn[src_tok]
    y_sorted = grouped_swiglu(x_sorted, wg.astype(BF16), wu.astype(BF16), wd.astype(BF16),
                              tile_expert, n_valid, cw.reshape(rows, 1), None, tf=512)
    pos = pos.reshape(n, TOP_K)
    return res + y_sorted[pos[:, 0]] + y_sorted[pos[:, 1]]


SB_QT = 128
SB_KT = 128
SB_UNROLL = 4


def _sb_prompt_kernel(q_ref, k_ref, v_ref, u_ref, o_ref, acc_ref, r_ref):
    qi = pl.program_id(2)
    half = LANES // 2
    lane = lax.broadcasted_iota(jnp.int32, (SB_QT, LANES), 1)
    q = q_ref[...] * jnp.asarray(HEAD_DIM ** -0.5, BF16)
    zero = jnp.zeros_like(q)
    q_heads = (jnp.where(lane < half, q, zero), jnp.where(lane >= half, q, zero))
    u = u_ref[...]
    acc_ref[...] = jnp.zeros_like(acc_ref)
    r_ref[...] = jnp.zeros_like(r_ref)
    row = lax.broadcasted_iota(jnp.int32, (SB_QT, SB_KT), 0)
    col = lax.broadcasted_iota(jnp.int32, (SB_QT, SB_KT), 1)

    def tiles(kts, masked):
        parts = []
        for kt in kts:
            start = pl.multiple_of(kt * SB_KT, SB_KT)
            k = k_ref[pl.ds(start, SB_KT), :]
            v = v_ref[pl.ds(start, SB_KT), :]
            before = ((kt * SB_KT + col) < (qi * SB_QT + row)) if masked else None
            for hh in range(2):
                z = lax.dot_general(q_heads[hh], k, (((1,), (1,)), ((), ())), preferred_element_type=F32)
                l = jnp.log(1.0 + jnp.exp(-jnp.abs(z)))
                log_keep = -(jnp.maximum(z, 0.0) + l)
                if masked:
                    log_keep = jnp.where(before, log_keep, 0.0)
                hi = log_keep.astype(BF16)
                lo = (log_keep - hi.astype(F32)).astype(BF16)
                cs = jnp.dot(jnp.concatenate([hi, lo], axis=1), u, preferred_element_type=F32)
                parts.append((hh, z, cs, before, v))
        r = [r_ref[0], r_ref[1]]
        total = None
        for t in range(len(kts)):
            pv = []
            for hh, z, cs, before, v in parts[2 * t:2 * t + 2]:
                a = jnp.exp(z + cs[:, :SB_KT] + r[hh])
                if masked:
                    a = jnp.where(before, a, 0.0)
                r[hh] = r[hh] + cs[:, SB_KT:]
                pv.append(jnp.dot(a.astype(BF16), v, preferred_element_type=F32))
            both = jnp.where(lane < half, pv[0], pv[1])
            total = both if total is None else total + both
        r_ref[0] = r[0]
        r_ref[1] = r[1]
        acc_ref[...] += total

    n_diag = SB_QT // SB_KT
    n_full = qi * n_diag
    tiles([n_full + n_diag - 1 - d for d in range(n_diag)], True)
    rem = n_full % SB_UNROLL

    def body_group(i, carry):
        first = n_full - 1 - i * SB_UNROLL
        tiles([first - d for d in range(SB_UNROLL)], False)
        return carry

    lax.fori_loop(0, n_full // SB_UNROLL, body_group, 0)

    @pl.when(rem >= 2)
    def _():
        tiles([rem - 1, rem - 2], False)

    @pl.when(rem % 2 == 1)
    def _():
        tiles([0], False)
    o_ref[...] = acc_ref[...].astype(o_ref.dtype)


def sb_prompt_attention(qkv, batch, seq):
    d = SB_HEADS * HEAD_DIM
    n_pair = d // LANES
    nq = seq // SB_QT
    jj = np.arange(SB_KT)
    u = np.concatenate([(jj[:, None] >= jj[None, :]).astype(np.float32), np.ones((SB_KT, SB_KT), np.float32)], axis=1)
    u = np.concatenate([u, u], axis=0)
    return pl.pallas_call(
        _sb_prompt_kernel,
        grid=(batch, n_pair, nq),
        in_specs=[pl.BlockSpec((SB_QT, LANES), lambda b, p, i: (b * nq + i, p)),
                  pl.BlockSpec((seq, LANES), lambda b, p, i: (b, n_pair + p)),
                  pl.BlockSpec((seq, LANES), lambda b, p, i: (b, 2 * n_pair + p)),
                  pl.BlockSpec((2 * SB_KT, 2 * SB_KT), lambda b, p, i: (0, 0))],
        out_specs=pl.BlockSpec((SB_QT, LANES), lambda b, p, i: (b * nq + i, p)),
        out_shape=jax.ShapeDtypeStruct((batch * seq, d), BF16),
        scratch_shapes=[pltpu.VMEM((SB_QT, LANES), F32), pltpu.VMEM((2, SB_QT, SB_KT), F32)],
        compiler_params=_cparams(("parallel", "parallel", "arbitrary")),
        name="sb_prompt_attention",
    )(qkv, qkv, qkv, jnp.asarray(u, BF16))


NSA_QT = 128
NSA_KT = 512


def _nsa_first_tile(qi, window):
    if not window:
        return 0
    return jnp.maximum(qi * NSA_QT - (WINDOW - 1), 0) // NSA_KT


def _nsa_attn_kernel(*refs, window):
    if window:
        q_ref, k_ref, v_ref, o_ref, m_ref, l_ref, acc_ref = refs
    else:
        q_ref, k_ref, v_ref, sel_ref, e_ref, o_ref, m_ref, l_ref, acc_ref = refs
    qi = pl.program_id(2)
    step = pl.program_id(3)
    kt = _nsa_first_tile(qi, window) + step
    width = NSA_GROUP * HEAD_DIM
    lane_head = lax.broadcasted_iota(jnp.int32, (NSA_QT, width), 1) // HEAD_DIM

    @pl.when(step == 0)
    def _():
        m_ref[...] = jnp.full_like(m_ref, NEG_BIG)
        l_ref[...] = jnp.zeros_like(l_ref)
        acc_ref[...] = jnp.zeros_like(acc_ref)

    @pl.when(kt * NSA_KT <= qi * NSA_QT + NSA_QT - 1)
    def _():
        q = q_ref[...] * jnp.asarray(HEAD_DIM ** -0.5, BF16)
        zero = jnp.zeros_like(q)
        row = lax.broadcasted_iota(jnp.int32, (NSA_QT, LANES), 0)
        col = lax.broadcasted_iota(jnp.int32, (NSA_QT, LANES), 1)
        for j in range(NSA_KT // LANES):
            k = k_ref[j * LANES:(j + 1) * LANES, :]
            v = v_ref[j * LANES:(j + 1) * LANES, :]
            dist = (qi * NSA_QT + row) - (kt * NSA_KT + j * LANES + col)
            if window:
                valid = jnp.logical_and(dist >= 0, dist < WINDOW)
            else:
                chosen = jnp.dot(sel_ref[...], e_ref[:, j * LANES:(j + 1) * LANES], preferred_element_type=F32)
                valid = jnp.logical_and(chosen > 0.5, dist >= 0)
            for r in range(NSA_GROUP):
                qr = jnp.where(lane_head == r, q, zero)
                s = lax.dot_general(qr, k, (((1,), (1,)), ((), ())), preferred_element_type=F32)
                s = jnp.where(valid, s, NEG_BIG)
                m_prev = m_ref[r]
                m_new = jnp.maximum(m_prev, jnp.max(s, axis=-1, keepdims=True))
                alpha = jnp.exp(m_prev - m_new)
                p = jnp.where(valid, jnp.exp(s - m_new), 0.0)
                l_ref[r] = alpha * l_ref[r] + jnp.sum(p, axis=-1, keepdims=True)
                m_ref[r] = m_new
                pv = jnp.dot(p.astype(BF16), v, preferred_element_type=F32)
                acc_ref[r] = jnp.concatenate([alpha, alpha], axis=1) * acc_ref[r] + pv

    @pl.when(step == pl.num_programs(3) - 1)
    def _():
        out = jnp.zeros((NSA_QT, width), F32)
        for r in range(NSA_GROUP):
            denom = jnp.maximum(l_ref[r], TINY)
            o_r = acc_ref[r] / jnp.concatenate([denom, denom], axis=1)
            out = jnp.where(lane_head == r, o_r, out)
        o_ref[...] = out


def nsa_prompt_attention(q, k_rep, v_rep, sel, batch, seq):
    window = sel is None
    width = NSA_GROUP * HEAD_DIM
    nq = seq // NSA_QT
    nk = seq // NSA_KT
    first = [max(i * NSA_QT - (WINDOW - 1), 0) // NSA_KT if window else 0 for i in range(nq)]
    steps = max((i * NSA_QT + NSA_QT - 1) // NSA_KT - first[i] + 1 for i in range(nq))

    def key_tile(i, t):
        return jnp.minimum(_nsa_first_tile(i, window) + t, (i * NSA_QT + NSA_QT - 1) // NSA_KT)

    kv_spec = pl.BlockSpec((NSA_KT, width), lambda b, g, i, t: (b * nk + key_tile(i, t), g))
    in_specs = [pl.BlockSpec((NSA_QT, width), lambda b, g, i, t: (b * nq + i, g)), kv_spec, kv_spec]
    args = [q, k_rep, v_rep]
    if not window:
        n_sel = seq // SEL_BLOCK
        expand = (np.arange(n_sel)[:, None] == (np.arange(seq)[None, :] // SEL_BLOCK)).astype(np.float32)
        in_specs += [pl.BlockSpec((NSA_QT, n_sel), lambda b, g, i, t: ((b * NSA_KV_HEADS + g) * nq + i, 0)),
                     pl.BlockSpec((n_sel, NSA_KT), lambda b, g, i, t: (0, key_tile(i, t)))]
        args += [sel, jnp.asarray(expand, BF16)]
    return pl.pallas_call(
        functools.partial(_nsa_attn_kernel, window=window),
        grid=(batch, NSA_KV_HEADS, nq, steps),
        in_specs=in_specs,
        out_specs=pl.BlockSpec((NSA_QT, width), lambda b, g, i, t: (b * nq + i, g)),
        out_shape=jax.ShapeDtypeStruct((batch * seq, NSA_HEADS * HEAD_DIM), F32),
        scratch_shapes=[pltpu.VMEM((NSA_GROUP, NSA_QT, LANES), F32),
                        pltpu.VMEM((NSA_GROUP, NSA_QT, LANES), F32),
                        pltpu.VMEM((NSA_GROUP, NSA_QT, width), F32)],
        compiler_params=_cparams(("parallel", "parallel", "parallel", "arbitrary")),
        name="nsa_window_prompt" if window else "nsa_selected_prompt",
    )(*args)


def head_rmsnorm(h, g, dtype):
    y = h * lax.rsqrt(jnp.mean(h * h, axis=-1, keepdims=True) + EPS)
    return (y * g.astype(jnp.float32)).astype(dtype)


def masked_softmax(s, valid, axes):
    s = jnp.where(valid, s, NEG_BIG)
    m = jnp.max(s, axis=axes, keepdims=True)
    e = jnp.where(valid, jnp.exp(s - m), 0.0)
    return e / jnp.maximum(jnp.sum(e, axis=axes, keepdims=True), TINY)


def split_cols(h, sizes):
    offs = [int(o) for o in np.cumsum(sizes)[:-1]]
    return jnp.split(h, offs, axis=-1)


def _repeat_group(a):
    B, T, G, dh = a.shape
    return jnp.tile(a.astype(BF16)[:, :, :, None, :], (1, 1, 1, NSA_GROUP, 1)).reshape(B * T, G * NSA_GROUP * dh)


def nsa_compress(rows, pos_emb, w):
    B, Tk, G, dh = rows.shape
    blk = rows.reshape(B, Tk // CMP_BLOCK, CMP_BLOCK, G, dh) + pos_emb[None, None, :, None, :]
    blk = blk.transpose(0, 1, 3, 2, 4).reshape(B, Tk // CMP_BLOCK, G, CMP_BLOCK * dh)
    return blk @ w


def nsa_compressed_selected(q, rows, q_pos, cmp_pos, cmp_w):
    B, Tq, H, dh = q.shape
    G, R = NSA_KV_HEADS, NSA_GROUP
    pad = (-rows.shape[1]) % SEL_BLOCK
    rows = jnp.pad(rows, ((0, 0), (0, pad), (0, 0), (0, 0), (0, 0)))
    Tp = rows.shape[1]
    scale = dh ** -0.5
    qg = q.reshape(B, Tq, G, R, dh)
    k_cmp = nsa_compress(rows[:, :, 0], cmp_pos[0], cmp_w[0])
    v_cmp = nsa_compress(rows[:, :, 1], cmp_pos[1], cmp_w[1])
    n_cmp = Tp // CMP_BLOCK
    s = jnp.einsum('btgrd,bngd->bgrtn', qg, k_cmp).astype(jnp.float32) * scale
    cmp_end = (jnp.arange(n_cmp) + 1) * CMP_BLOCK - 1
    p = masked_softmax(s, cmp_end[None, :] <= q_pos[:, None], -1)
    o_cmp = jnp.einsum('bgrtn,bngd->btgrd', p.astype(v_cmp.dtype), v_cmp).reshape(B, Tq, H, dh)
    n_sel = Tp // SEL_BLOCK
    imp = p.sum(axis=2).reshape(B, G, Tq, n_sel, SEL_BLOCK // CMP_BLOCK).sum(axis=-1)
    blk = jnp.arange(n_sel)
    forced = (blk[None, :] == 0) | (blk[None, :] == (q_pos // SEL_BLOCK)[:, None])
    avail = blk[None, :] * SEL_BLOCK <= q_pos[:, None]
    imp = jnp.where(avail, jnp.where(forced, FORCED_SCORE, imp), -1.0)
    _, idx = lax.top_k(imp, min(TOP_N, n_sel))
    n_top = idx.shape[-1]
    if Tq == Tp and Tq % NSA_KT == 0:
        sel = jnp.sum(jax.nn.one_hot(idx, n_sel, dtype=BF16), axis=-2).reshape(B * G * Tq, n_sel)
        o_sel = nsa_prompt_attention(q.astype(BF16).reshape(B * Tq, H * dh), _repeat_group(rows[:, :, 2]),
                                     _repeat_group(rows[:, :, 3]), sel, B, Tq)
        return o_cmp, o_sel.reshape(B, Tq, H, dh)
    ks = rows[:, :, 2].reshape(B, n_sel, SEL_BLOCK, G, dh).transpose(0, 3, 1, 2, 4)
    vs = rows[:, :, 3].reshape(B, n_sel, SEL_BLOCK, G, dh).transpose(0, 3, 1, 2, 4)
    qb = SEL_QUERY_BLOCK if Tq % SEL_QUERY_BLOCK == 0 else Tq
    nqb = Tq // qb
    q_blocks = qg.reshape(B, nqb, qb, G, R, dh).swapaxes(0, 1)
    idx_blocks = idx.reshape(B, G, nqb, qb, n_top).transpose(2, 0, 1, 3, 4)
    pos_blocks = q_pos.reshape(nqb, qb)
    b_ix = jnp.arange(B)[:, None, None, None]
    g_ix = jnp.arange(G)[None, :, None, None]

    def sel_block(args):
        qc, ic, pc = args
        kg = ks[b_ix, g_ix, ic]
        vg = vs[b_ix, g_ix, ic]
        sc = jnp.einsum('bqgrd,bgqnkd->bgrqnk', qc, kg).astype(jnp.float32) * scale
        kpos = ic[..., None] * SEL_BLOCK + jnp.arange(SEL_BLOCK)
        ok = (kpos <= pc[None, None, :, None, None])[:, :, None]
        w = masked_softmax(sc, ok, (-2, -1))
        return jnp.einsum('bgrqnk,bgqnkd->bqgrd', w.astype(vg.dtype), vg)

    o_sel = lax.map(sel_block, (q_blocks, idx_blocks, pos_blocks))
    o_sel = o_sel.swapaxes(0, 1).reshape(B, Tq, H, dh)
    return o_cmp, o_sel


def window_core(qg, k, v, q_pos, k_pos):
    s = jnp.einsum('btgrd,bkgd->bgrtk', qg, k).astype(jnp.float32) * (qg.shape[-1] ** -0.5)
    dist = q_pos[:, None] - k_pos[None, :]
    ok = (dist >= 0) & (dist < WINDOW) & (k_pos[None, :] >= 0)
    w = masked_softmax(s, ok, -1)
    return jnp.einsum('bgrtk,bkgd->btgrd', w.astype(v.dtype), v)


def window_banded(qg, k, v):
    B, T = qg.shape[:2]
    qb = QUERY_BLOCK if T % QUERY_BLOCK == 0 else T
    nqb = T // qb
    span = WINDOW + qb
    kidx = jnp.arange(nqb)[:, None] * qb + jnp.arange(span)[None, :]
    pad = ((0, 0), (WINDOW, 0), (0, 0), (0, 0))
    kb = jnp.pad(k, pad)[:, kidx]
    vb = jnp.pad(v, pad)[:, kidx]
    q_pos = jnp.arange(T).reshape(nqb, qb)
    k_pos = kidx - WINDOW
    q_blocks = qg.reshape(B, nqb, qb, *qg.shape[2:])
    o = jax.vmap(window_core, in_axes=(1, 1, 1, 0, 0), out_axes=1)(q_blocks, kb, vb, q_pos, k_pos)
    return o.reshape(qg.shape)


def mlstm_chunkwise(q, k, v, i_pre, f_pre, C0, n0, m0):
    B, T, H, d = q.shape
    L = MLSTM_CHUNK if T % MLSTM_CHUNK == 0 else T
    nc = T // L
    f32 = jnp.float32
    q, k, v = q.astype(f32), k.astype(f32) * (d ** -0.5), v.astype(f32)
    log_f = jax.nn.log_sigmoid(f_pre.astype(f32))
    i_pre = i_pre.astype(f32)
    chunks = lambda a: a.reshape(B, nc, L, *a.shape[2:]).swapaxes(0, 1)
    causal = jnp.tril(jnp.ones((L, L), dtype=bool))

    def step(carry, xs):
        C, n, m = carry
        qc, kc, vc, ic, fc = xs
        b = jnp.cumsum(fc, axis=1).swapaxes(1, 2)
        it = ic.swapaxes(1, 2)
        log_d = jnp.where(causal, b[..., :, None] - b[..., None, :] + it[..., None, :], -jnp.inf)
        m_inter = b + m[..., None]
        m_t = jnp.maximum(m_inter, jnp.max(log_d, axis=-1))
        w = jnp.einsum('blhd,bshd->bhls', qc, kc) * jnp.exp(log_d - m_t[..., None])
        carry_w = jnp.exp(m_inter - m_t)
        num = jnp.einsum('bhls,bshd->blhd', w, vc) + jnp.einsum('blhd,bhde->blhe', qc, C) * carry_w.swapaxes(1, 2)[..., None]
        den = jnp.sum(w, axis=-1) + jnp.einsum('blhd,bhd->bhl', qc, n) * carry_w
        den = jnp.maximum(jnp.abs(den), jnp.exp(-m_t))
        h = num / den.swapaxes(1, 2)[..., None]
        m_new = m_t[..., -1]
        w_end = jnp.exp(b[..., -1:] - b + it - m_new[..., None])
        decay = jnp.exp(b[..., -1] + m - m_new)
        C_new = decay[..., None, None] * C + jnp.einsum('bhs,bshd,bshe->bhde', w_end, kc, vc)
        n_new = decay[..., None] * n + jnp.einsum('bhs,bshd->bhd', w_end, kc)
        return (C_new, n_new, m_new), h

    (C, n, m), hs = lax.scan(step, (C0.astype(f32), n0.astype(f32), m0.astype(f32)),
                             tuple(chunks(a) for a in (q, k, v, i_pre, log_f)))
    return hs.swapaxes(0, 1).reshape(B, T, H, d), C, n, m


def nsa_mlstm_core(proj, start, past_rows, win_buf, C0, n0, m0, nsa_cmp_pos, nsa_cmp_w,
                   mlstm_b_i, mlstm_b_f, mlstm_norm):
    B, T, _ = proj.shape
    (q_a, k_c, v_c, k_s, v_s, k_w, v_w, g_a, q_m, k_m, v_m, o_m, i_m, f_m) = split_cols(proj, IN0_SIZES)
    kvh = lambda a: a.reshape(B, T, NSA_KV_HEADS, HEAD_DIM)
    mh = lambda a: a.reshape(B, T, MLSTM_HEADS, MLSTM_DIM)
    q_a = q_a.reshape(B, T, NSA_HEADS, HEAD_DIM)
    q_pos = start + jnp.arange(T)
    new_rows = jnp.stack([kvh(k_c), kvh(v_c), kvh(k_s), kvh(v_s)], axis=2)
    rows_full = jnp.concatenate([past_rows.astype(new_rows.dtype), new_rows], axis=1)
    o_cmp, o_sel = nsa_compressed_selected(q_a, rows_full, q_pos, nsa_cmp_pos, nsa_cmp_w)
    win_rows = jnp.stack([kvh(k_w), kvh(v_w)], axis=2)
    qg = q_a.reshape(B, T, NSA_KV_HEADS, NSA_GROUP, HEAD_DIM)
    if win_buf is None:
        if T % NSA_KT == 0:
            o_win = nsa_prompt_attention(q_a.astype(BF16).reshape(B * T, NSA_HEADS * HEAD_DIM),
                                         _repeat_group(win_rows[:, :, 0]), _repeat_group(win_rows[:, :, 1]), None, B, T)
        else:
            o_win = window_banded(qg, win_rows[:, :, 0], win_rows[:, :, 1])
        new_win = win_rows[:, T - min(WINDOW, T):]
    else:
        n_buf = win_buf.shape[1]
        buf = jnp.concatenate([win_buf.astype(win_rows.dtype), win_rows], axis=1)
        k_pos = start - n_buf + jnp.arange(n_buf + T)
        o_win = window_core(qg, buf[:, :, 0], buf[:, :, 1], q_pos, k_pos)
        new_win = buf[:, T:]
    o_win = o_win.reshape(B, T, NSA_HEADS, HEAD_DIM)
    gates = jax.nn.sigmoid(g_a.reshape(B, T, NSA_HEADS, 3))
    o_a = gates[..., 0:1] * o_cmp + gates[..., 1:2] * o_sel + gates[..., 2:3] * o_win
    h_m, C, n, m = mlstm_chunkwise(mh(q_m), mh(k_m), mh(v_m), i_m + mlstm_b_i, f_m + mlstm_b_f, C0, n0, m0)
    h_m = head_rmsnorm(h_m, mlstm_norm, proj.dtype) * jax.nn.sigmoid(mh(o_m))
    mixed = jnp.concatenate([o_a.reshape(B, T, -1), h_m.reshape(B, T, -1)], axis=-1)
    return mixed, new_rows, new_win, C, n, m


def sb_attend(q, k, v, start):
    B, Tq, H, dh = q.shape
    Tk = k.shape[1]
    qb = QUERY_BLOCK if Tq % QUERY_BLOCK == 0 else Tq
    nqb = Tq // qb
    q_blocks = q.reshape(B, nqb, qb, H, dh).swapaxes(0, 1)
    pos_blocks = (start + jnp.arange(Tq)).reshape(nqb, qb)
    k_pos = jnp.arange(Tk)
    scale = dh ** -0.5

    def block(args):
        qc, pc = args
        z = jnp.einsum('bqhd,bkhd->bhqk', qc, k).astype(jnp.float32) * scale
        before = k_pos[None, :] < pc[:, None]
        log_keep = jnp.where(before, -jax.nn.softplus(z), 0.0)
        tail = lax.cumsum(log_keep, axis=3, reverse=True) - log_keep
        a = jnp.where(before, jnp.exp(jax.nn.log_sigmoid(z) + tail), 0.0)
        return jnp.einsum('bhqk,bkhd->bqhd', a.astype(v.dtype), v)

    o = lax.map(block, (q_blocks, pos_blocks))
    return o.swapaxes(0, 1).reshape(B, Tq, H, dh)


def sb_core(qkv, start, past_k, past_v):
    k_full = jnp.concatenate([past_k.astype(qkv.dtype), qkv[:, :, 1]], axis=1)
    v_full = jnp.concatenate([past_v.astype(qkv.dtype), qkv[:, :, 2]], axis=1)
    o = sb_attend(qkv[:, :, 0], k_full, v_full, start)
    return o.reshape(o.shape[0], o.shape[1], -1), qkv[:, :, 1:]


def kernel(x_prompt, x_sample, cache_nsa_kv, state_nsa_win, state_mlstm_C, state_mlstm_n, state_mlstm_m, cache_sb_kv, page_table, norm_mix0, w_in0, nsa_cmp_pos, nsa_cmp_w, mlstm_b_i, mlstm_b_f, mlstm_norm, w_out0, norm_ffn0, w_gate0, w_up0, w_down0, norm_mix1, w_qkv1, w_out1, norm_ffn1, w_router1, w_gate1, w_up1, w_down1, norm_final):
    B, T, D = x_prompt.shape
    DB, TS, _ = x_sample.shape
    n_p = B * T
    past_len = page_table.shape[1] * cache_nsa_kv.shape[1]
    x = jnp.concatenate([x_prompt.reshape(n_p, D), x_sample.reshape(DB * TS, D)], axis=0)

    hn = rmsnorm_tokens(x, norm_mix0, BF16)
    in0_pad = (-IN0_COLS) % (9 * LANES)
    w_in = jnp.pad(w_in0, ((0, 0), (0, in0_pad))).astype(BF16)
    proj = matmul(hn, w_in, tn=(IN0_COLS + in0_pad) // 3)[:, :IN0_COLS]
    proj_p = proj[:n_p].reshape(B, T, IN0_COLS)
    proj_s = proj[n_p:].reshape(DB, TS, IN0_COLS)
    empty = jnp.zeros((B, 0, 4, NSA_KV_HEADS, HEAD_DIM), F32)
    c0 = jnp.zeros((B, MLSTM_HEADS, MLSTM_DIM, MLSTM_DIM), F32)
    n0 = jnp.zeros((B, MLSTM_HEADS, MLSTM_DIM), F32)
    m0 = jnp.zeros((B, MLSTM_HEADS), F32)
    mixed_p, nsa_rows_p, nsa_win_p, mC_p, mn_p, mm_p = nsa_mlstm_core(
        proj_p, 0, empty, None, c0, n0, m0, nsa_cmp_pos, nsa_cmp_w, mlstm_b_i, mlstm_b_f, mlstm_norm)
    past_rows = cache_nsa_kv[page_table].reshape(DB, past_len, 4, NSA_KV_HEADS, HEAD_DIM)
    mixed_s, nsa_rows_s, nsa_win_s, mC_s, mn_s, mm_s = nsa_mlstm_core(
        proj_s, past_len, past_rows, state_nsa_win, state_mlstm_C, state_mlstm_n, state_mlstm_m,
        nsa_cmp_pos, nsa_cmp_w, mlstm_b_i, mlstm_b_f, mlstm_norm)
    mixed = jnp.concatenate([mixed_p.reshape(n_p, D), mixed_s.reshape(DB * TS, D)], axis=0)
    x = matmul(mixed.astype(BF16), w_out0.astype(BF16), res=x)
    x = dense_ffn(rmsnorm_tokens(x, norm_ffn0, BF16), x, w_gate0, w_up0, w_down0)

    hn = rmsnorm_tokens(x, norm_mix1, BF16)
    qkv, qkv_b = matmul(hn, w_qkv1.astype(BF16), also_bf16=True)
    sb_rows_p = qkv[:n_p].reshape(B, T, 3, SB_HEADS, HEAD_DIM)[:, :, 1:]
    qkv_s = qkv[n_p:].reshape(DB, TS, 3, SB_HEADS, HEAD_DIM)
    o_p = sb_prompt_attention(qkv_b, B, T)
    past_k = cache_sb_kv[page_table, :, 0].reshape(DB, past_len, SB_HEADS, HEAD_DIM)
    past_v = cache_sb_kv[page_table, :, 1].reshape(DB, past_len, SB_HEADS, HEAD_DIM)
    o_s, sb_rows_s = sb_core(qkv_s, past_len, past_k, past_v)
    o = jnp.concatenate([o_p, o_s.reshape(DB * TS, D).astype(BF16)], axis=0)
    x = matmul(o, w_out1.astype(BF16), res=x)
    hn, logits = rmsnorm_router(x, norm_ffn1, w_router1)
    x = moe_ffn(hn, logits[:, :N_EXPERTS], x, w_gate1, w_up1, w_down1)

    y = rmsnorm_tokens(x, norm_final, F32)
    y_prompt = y[:n_p].reshape(B, T, D)
    y_sample = y[n_p:].reshape(DB, TS, D)
    return (y_prompt, y_sample, nsa_rows_p, nsa_rows_s, nsa_win_p, nsa_win_s, mC_p, mn_p, mm_p,
            mC_s, mn_s, mm_s, sb_rows_p, sb_rows_s)
```

```python
import functools

import numpy as np
import jax
import jax.numpy as jnp
from jax import lax
from jax.experimental import pallas as pl
from jax.experimental.pallas import tpu as pltpu

D_MODEL = 1024
HEAD_DIM = 64
NSA_HEADS = 8
NSA_KV_HEADS = 2
NSA_GROUP = NSA_HEADS // NSA_KV_HEADS
CMP_BLOCK = 32
SEL_BLOCK = 64
TOP_N = 16
WINDOW = 512
SEL_QUERY_BLOCK = 64
FORCED_SCORE = 1e4
MLSTM_HEADS = 4
MLSTM_DIM = 128
MLSTM_CHUNK = 64
SB_HEADS = D_MODEL // HEAD_DIM
QUERY_BLOCK = 128
N_EXPERTS = 8
TOP_K = 2
EPS = 1e-6
NEG_BIG = -1e30
TINY = 1e-30

IN0_SIZES = (NSA_HEADS * HEAD_DIM,) + (NSA_KV_HEADS * HEAD_DIM,) * 6 + (3 * NSA_HEADS,) + (MLSTM_HEADS * MLSTM_DIM,) * 4 + (MLSTM_HEADS, MLSTM_HEADS)
IN0_COLS = sum(IN0_SIZES)

LANES = 128
TOKEN_TILE = 512
VMEM_LIMIT = 48 * 1024 * 1024
BF16 = jnp.bfloat16
F32 = jnp.float32


def _cparams(sem):
    return pltpu.CompilerParams(dimension_semantics=sem, vmem_limit_bytes=VMEM_LIMIT)


def _rmsnorm_kernel(x_ref, g_ref, o_ref):
    x = x_ref[...]
    y = x * lax.rsqrt(jnp.mean(x * x, axis=-1, keepdims=True) + EPS)
    o_ref[...] = (y * g_ref[...]).astype(o_ref.dtype)


def rmsnorm_tokens(x, g, out_dtype):
    n, d = x.shape
    return pl.pallas_call(
        _rmsnorm_kernel,
        grid=(n // TOKEN_TILE,),
        in_specs=[pl.BlockSpec((TOKEN_TILE, d), lambda i: (i, 0)),
                  pl.BlockSpec((1, d), lambda i: (0, 0))],
        out_specs=pl.BlockSpec((TOKEN_TILE, d), lambda i: (i, 0)),
        out_shape=jax.ShapeDtypeStruct((n, d), out_dtype),
        compiler_params=_cparams(("parallel",)),
        name="rmsnorm",
    )(x, g.reshape(1, d))


def _rmsnorm_router_kernel(x_ref, g_ref, wr_ref, o_ref, logit_ref):
    x = x_ref[...]
    y = x * lax.rsqrt(jnp.mean(x * x, axis=-1, keepdims=True) + EPS) * g_ref[...]
    o_ref[...] = y.astype(o_ref.dtype)
    logit_ref[...] = jnp.dot(y, wr_ref[...], precision=lax.Precision.HIGHEST,
                             preferred_element_type=F32)


def rmsnorm_router(x, g, w_router):
    n, d = x.shape
    wr = jnp.pad(w_router, ((0, 0), (0, LANES - w_router.shape[1])))
    return pl.pallas_call(
        _rmsnorm_router_kernel,
        grid=(n // TOKEN_TILE,),
        in_specs=[pl.BlockSpec((TOKEN_TILE, d), lambda i: (i, 0)),
                  pl.BlockSpec((1, d), lambda i: (0, 0)),
                  pl.BlockSpec((d, LANES), lambda i: (0, 0))],
        out_specs=[pl.BlockSpec((TOKEN_TILE, d), lambda i: (i, 0)),
                   pl.BlockSpec((TOKEN_TILE, LANES), lambda i: (i, 0))],
        out_shape=[jax.ShapeDtypeStruct((n, d), BF16),
                   jax.ShapeDtypeStruct((n, LANES), F32)],
        compiler_params=_cparams(("parallel",)),
        name="rmsnorm_router",
    )(x, g.reshape(1, d), wr)


def _matmul_kernel(a_ref, w_ref, o_ref):
    o_ref[...] = jnp.dot(a_ref[...], w_ref[...], preferred_element_type=F32)


def _matmul_dual_kernel(a_ref, w_ref, o_ref, ob_ref):
    y = jnp.dot(a_ref[...], w_ref[...], preferred_element_type=F32)
    o_ref[...] = y
    ob_ref[...] = y.astype(BF16)


def _matmul_res_kernel(a_ref, w_ref, r_ref, o_ref):
    o_ref[...] = r_ref[...] + jnp.dot(a_ref[...], w_ref[...], preferred_element_type=F32)


def matmul(a, w, res=None, tn=1024, also_bf16=False):
    n, k = a.shape
    m = w.shape[1]
    tn = min(tn, m)
    assert n % TOKEN_TILE == 0 and m % tn == 0
    assert not (also_bf16 and res is not None)
    in_specs = [pl.BlockSpec((TOKEN_TILE, k), lambda i, j: (i, 0)),
                pl.BlockSpec((k, tn), lambda i, j: (0, j))]
    args = [a, w]
    body = _matmul_kernel
    out_spec = pl.BlockSpec((TOKEN_TILE, tn), lambda i, j: (i, j))
    out_specs = out_spec
    out_shape = jax.ShapeDtypeStruct((n, m), F32)
    if res is not None:
        in_specs.append(pl.BlockSpec((TOKEN_TILE, tn), lambda i, j: (i, j)))
        args.append(res)
        body = _matmul_res_kernel
    if also_bf16:
        body = _matmul_dual_kernel
        out_specs = [out_spec, out_spec]
        out_shape = [out_shape, jax.ShapeDtypeStruct((n, m), BF16)]
    return pl.pallas_call(
        body,
        grid=(n // TOKEN_TILE, m // tn),
        in_specs=in_specs,
        out_specs=out_specs,
        out_shape=out_shape,
        compiler_params=_cparams(("parallel", "parallel")),
        name="matmul",
    )(*args)


def _swiglu_kernel(te_ref, nv_ref, x_ref, wg_ref, wu_ref, wd_ref, cw_ref, *rest, has_res):
    if has_res:
        r_ref, o_ref, acc_ref = rest
    else:
        o_ref, acc_ref = rest
    i = pl.program_id(0)
    j = pl.program_id(1)

    @pl.when(j == 0)
    def _():
        acc_ref[...] = jnp.zeros_like(acc_ref)

    @pl.when(i < nv_ref[0])
    def _():
        x = x_ref[...]
        g = jnp.dot(x, wg_ref[0], preferred_element_type=F32)
        u = jnp.dot(x, wu_ref[0], preferred_element_type=F32)
        h = (g * jax.nn.sigmoid(g)) * u
        acc_ref[...] += jnp.dot(h.astype(BF16), wd_ref[0], preferred_element_type=F32)

    @pl.when(j == pl.num_programs(1) - 1)
    def _():
        y = acc_ref[...] * cw_ref[...]
        if has_res:
            y = y + r_ref[...]
        o_ref[...] = y


def grouped_swiglu(x, wg, wu, wd, tile_expert, n_valid, cw, res, tf):
    r, d = x.shape
    f = wg.shape[2]
    assert r % TOKEN_TILE == 0 and f % tf == 0
    has_res = res is not None
    in_specs = [
        pl.BlockSpec((TOKEN_TILE, d), lambda i, j, te, nv: (i, 0)),
        pl.BlockSpec((1, d, tf), lambda i, j, te, nv: (te[i], 0, j)),
        pl.BlockSpec((1, d, tf), lambda i, j, te, nv: (te[i], 0, j)),
        pl.BlockSpec((1, tf, d), lambda i, j, te, nv: (te[i], j, 0)),
        pl.BlockSpec((TOKEN_TILE, 1), lambda i, j, te, nv: (i, 0)),
    ]
    args = [x, wg, wu, wd, cw]
    if has_res:
        in_specs.append(pl.BlockSpec((TOKEN_TILE, d), lambda i, j, te, nv: (i, 0)))
        args.append(res)
    grid_spec = pltpu.PrefetchScalarGridSpec(
        num_scalar_prefetch=2,
        grid=(r // TOKEN_TILE, f // tf),
        in_specs=in_specs,
        out_specs=pl.BlockSpec((TOKEN_TILE, d), lambda i, j, te, nv: (i, 0)),
        scratch_shapes=[pltpu.VMEM((TOKEN_TILE, d), F32)],
    )
    return pl.pallas_call(
        functools.partial(_swiglu_kernel, has_res=has_res),
        grid_spec=grid_spec,
        out_shape=jax.ShapeDtypeStruct((r, d), F32),
        compiler_params=_cparams(("parallel", "arbitrary")),
        name="grouped_swiglu",
    )(tile_expert, n_valid, *args)


def dense_ffn(xn, res, wg, wu, wd):
    n = xn.shape[0]
    tiles = n // TOKEN_TILE
    return grouped_swiglu(xn, wg[None].astype(BF16), wu[None].astype(BF16), wd[None].astype(BF16),
                          jnp.zeros((tiles,), jnp.int32), jnp.full((1,), tiles, jnp.int32),
                          jnp.ones((n, 1), F32), res, tf=256)


def moe_ffn(xn, logits, res, wg, wu, wd):
    n, d = xn.shape
    probs = jax.nn.softmax(logits, axis=-1)
    top_w, top_i = lax.top_k(probs, TOP_K)
    top_w = top_w / jnp.sum(top_w, axis=-1, keepdims=True)
    flat_e = top_i.reshape(-1)
    order = jnp.argsort(flat_e, stable=True)
    sorted_e = flat_e[order]
    counts = jnp.sum(jax.nn.one_hot(flat_e, N_EXPERTS, dtype=jnp.int32), axis=0)
    padded = ((counts + TOKEN_TILE - 1) // TOKEN_TILE) * TOKEN_TILE
    ends_p = jnp.cumsum(padded)
    starts_p = ends_p - padded
    starts = jnp.cumsum(counts) - counts
    s_ix = jnp.arange(TOP_K * n, dtype=jnp.int32)
    dest = starts_p[sorted_e] + (s_ix - starts[sorted_e])
    rows = TOP_K * n + N_EXPERTS * TOKEN_TILE
    src_tok = jnp.zeros((rows,), jnp.int32).at[dest].set(order // TOP_K)
    cw = jnp.zeros((rows,), F32).at[dest].set(top_w.reshape(-1)[order])
    pos = jnp.zeros((TOP_K * n,), jnp.int32).at[order].set(dest)
    tiles = rows // TOKEN_TILE
    tile_start = jnp.arange(tiles, dtype=jnp.int32) * TOKEN_TILE
    tile_expert = jnp.minimum(jnp.searchsorted(ends_p, tile_start, side="right"), N_EXPERTS - 1).astype(jnp.int32)
    n_valid = (ends_p[-1] // TOKEN_TILE).astype(jnp.int32).reshape(1)
    x_sorted = xn[src_tok]
    y_sorted = grouped_swiglu(x_sorted, wg.astype(BF16), wu.astype(BF16), wd.astype(BF16),
                              tile_expert, n_valid, cw.reshape(rows, 1), None, tf=512)
    pos = pos.reshape(n, TOP_K)
    return res + y_sorted[pos[:, 0]] + y_sorted[pos[:, 1]]


SB_QT = 128
SB_KT = 128
SB_UNROLL = 4


def _sb_prompt_kernel(q_ref, k_ref, v_ref, u_ref, o_ref, acc_ref, r_ref):
    qi = pl.program_id(2)
    half = LANES // 2
    lane = lax.broadcasted_iota(jnp.int32, (SB_QT, LANES), 1)
    q = q_ref[...] * jnp.asarray(HEAD_DIM ** -0.5, BF16)
    zero = jnp.zeros_like(q)
    q_heads = (jnp.where(lane < half, q, zero), jnp.where(lane >= half, q, zero))
    u = u_ref[...]
    acc_ref[...] = jnp.zeros_like(acc_ref)
    r_ref[...] = jnp.zeros_like(r_ref)
    row = lax.broadcasted_iota(jnp.int32, (SB_QT, SB_KT), 0)
    col = lax.broadcasted_iota(jnp.int32, (SB_QT, SB_KT), 1)

    def tiles(kts, masked):
        parts = []
        for kt in kts:
            start = pl.multiple_of(kt * SB_KT, SB_KT)
            k = k_ref[pl.ds(start, SB_KT), :]
            v = v_ref[pl.ds(start, SB_KT), :]
            before = ((kt * SB_KT + col) < (qi * SB_QT + row)) if masked else None
            for hh in range(2):
                z = lax.dot_general(q_heads[hh], k, (((1,), (1,)), ((), ())), preferred_element_type=F32)
                l = jnp.log(1.0 + jnp.exp(-jnp.abs(z)))
                log_keep = -(jnp.maximum(z, 0.0) + l)
                if masked:
                    log_keep = jnp.where(before, log_keep, 0.0)
                hi = log_keep.astype(BF16)
                lo = (log_keep - hi.astype(F32)).astype(BF16)
                cs = jnp.dot(jnp.concatenate([hi, lo], axis=1), u, preferred_element_type=F32)
                parts.append((hh, z, cs, before, v))
        r = [r_ref[0], r_ref[1]]
        total = None
        for t in range(len(kts)):
            pv = []
            for hh, z, cs, before, v in parts[2 * t:2 * t + 2]:
                a = jnp.exp(z + cs[:, :SB_KT] + r[hh])
                if masked:
                    a = jnp.where(before, a, 0.0)
                r[hh] = r[hh] + cs[:, SB_KT:]
                pv.append(jnp.dot(a.astype(BF16), v, preferred_element_type=F32))
            both = jnp.where(lane < half, pv[0], pv[1])
            total = both if total is None else total + both
        r_ref[0] = r[0]
        r_ref[1] = r[1]
        acc_ref[...] += total

    n_diag = SB_QT // SB_KT
    n_full = qi * n_diag
    tiles([n_full + n_diag - 1 - d for d in range(n_diag)], True)
    rem = n_full % SB_UNROLL

    def body_group(i, carry):
        first = n_full - 1 - i * SB_UNROLL
        tiles([first - d for d in range(SB_UNROLL)], False)
        return carry

    lax.fori_loop(0, n_full // SB_UNROLL, body_group, 0)

    @pl.when(rem >= 2)
    def _():
        tiles([rem - 1, rem - 2], False)

    @pl.when(rem % 2 == 1)
    def _():
        tiles([0], False)
    o_ref[...] = acc_ref[...].astype(o_ref.dtype)


def sb_prompt_attention(qkv, batch, seq):
    d = SB_HEADS * HEAD_DIM
    n_pair = d // LANES
    nq = seq // SB_QT
    jj = np.arange(SB_KT)
    u = np.concatenate([(jj[:, None] >= jj[None, :]).astype(np.float32), np.ones((SB_KT, SB_KT), np.float32)], axis=1)
    u = np.concatenate([u, u], axis=0)
    return pl.pallas_call(
        _sb_prompt_kernel,
        grid=(batch, n_pair, nq),
        in_specs=[pl.BlockSpec((SB_QT, LANES), lambda b, p, i: (b * nq + i, p)),
                  pl.BlockSpec((seq, LANES), lambda b, p, i: (b, n_pair + p)),
                  pl.BlockSpec((seq, LANES), lambda b, p, i: (b, 2 * n_pair + p)),
                  pl.BlockSpec((2 * SB_KT, 2 * SB_KT), lambda b, p, i: (0, 0))],
        out_specs=pl.BlockSpec((SB_QT, LANES), lambda b, p, i: (b * nq + i, p)),
        out_shape=jax.ShapeDtypeStruct((batch * seq, d), BF16),
        scratch_shapes=[pltpu.VMEM((SB_QT, LANES), F32), pltpu.VMEM((2, SB_QT, SB_KT), F32)],
        compiler_params=_cparams(("parallel", "parallel", "arbitrary")),
        name="sb_prompt_attention",
    )(qkv, qkv, qkv, jnp.asarray(u, BF16))


def _sb_sample_kernel(pt_ref, q_ref, cache_ref, new_ref, u_ref, fold_ref, o_ref, acc_ref, r_ref):
    s = pl.program_id(1)
    rows, cols = acc_ref.shape
    n_new = rows // SB_HEADS

    @pl.when(s == 0)
    def _():
        acc_ref[...] = jnp.zeros_like(acc_ref)
        r_ref[...] = jnp.zeros_like(r_ref)

    def accumulate(kt, vt, valid):
        z = jnp.dot(q_ref[0], kt, preferred_element_type=F32)
        l = jnp.log(1.0 + jnp.exp(-jnp.abs(z)))
        log_keep = -(jnp.maximum(z, 0.0) + l)
        if valid is not None:
            log_keep = jnp.where(valid, log_keep, 0.0)
        hi = log_keep.astype(BF16)
        lo = (log_keep - hi.astype(F32)).astype(BF16)
        cs = jnp.dot(jnp.concatenate([hi, lo], axis=1), u_ref[...], preferred_element_type=F32)
        a = jnp.exp(z + cs[:, :LANES] + r_ref[...])
        if valid is not None:
            a = jnp.where(valid, a, 0.0)
        r_ref[...] += cs[:, LANES:]
        acc_ref[...] += lax.dot_general(a.astype(BF16), vt, (((1,), (1,)), ((), ())), preferred_element_type=F32)

    @pl.when(s == 0)
    def _():
        row = lax.broadcasted_iota(jnp.int32, (rows, LANES), 0)
        col = lax.broadcasted_iota(jnp.int32, (rows, LANES), 1)
        accumulate(new_ref[0, 0], new_ref[0, 1], col < row % n_new)

    @pl.when(s > 0)
    def _():
        kt = cache_ref[0, 0].reshape(cols, LANES).astype(BF16)
        vt = cache_ref[0, 1].reshape(cols, LANES).astype(BF16)
        accumulate(kt, vt, None)

    @pl.when(s == pl.num_programs(1) - 1)
    def _():
        row_head = lax.broadcasted_iota(jnp.int32, (rows, cols), 0) // n_new
        col_head = lax.broadcasted_iota(jnp.int32, (rows, cols), 1) // HEAD_DIM
        own = jnp.where(row_head == col_head, acc_ref[...], 0.0)
        o_ref[0] = jnp.dot(own, fold_ref[...], precision=lax.Precision.HIGHEST, preferred_element_type=F32)


def sb_sample_attention(qkv_s, cache_sb_kv, page_table):
    DB, TS, _, H, dh = qkv_s.shape
    page = cache_sb_kv.shape[1]
    n_pages = page_table.shape[1]
    assert page == LANES and TS <= LANES
    d = H * dh
    q = qkv_s[:, :, 0].astype(BF16) * jnp.asarray(dh ** -0.5, BF16)
    q_blocks = jnp.einsum('bthd,hg->bhtgd', q, jnp.eye(H, dtype=BF16)).reshape(DB, H * TS, d)
    new_t = jnp.transpose(qkv_s[:, :, 1:], (0, 2, 3, 4, 1)).reshape(DB, 2, d, TS).astype(BF16)
    new_t = jnp.pad(new_t, ((0, 0), (0, 0), (0, 0), (0, LANES - TS)))
    cache_t = jnp.transpose(cache_sb_kv, (0, 2, 3, 4, 1))
    jj = np.arange(LANES)
    u = np.concatenate([(jj[:, None] >= jj[None, :]).astype(np.float32), np.ones((LANES, LANES), np.float32)], axis=1)
    u = np.concatenate([u, u], axis=0)
    fold = (np.arange(d)[:, None] % dh == np.arange(dh)[None, :]).astype(np.float32)

    def page_map(b, s, pt):
        return (pt[b, n_pages - 1 - jnp.maximum(s - 1, 0)], 0, 0, 0, 0)

    grid_spec = pltpu.PrefetchScalarGridSpec(
        num_scalar_prefetch=1,
        grid=(DB, n_pages + 1),
        in_specs=[pl.BlockSpec((1, H * TS, d), lambda b, s, pt: (b, 0, 0)),
                  pl.BlockSpec((1, 2, H, dh, page), page_map),
                  pl.BlockSpec((1, 2, d, LANES), lambda b, s, pt: (b, 0, 0, 0)),
                  pl.BlockSpec((2 * LANES, 2 * LANES), lambda b, s, pt: (0, 0)),
                  pl.BlockSpec((d, dh), lambda b, s, pt: (0, 0))],
        out_specs=pl.BlockSpec((1, H * TS, dh), lambda b, s, pt: (b, 0, 0)),
        scratch_shapes=[pltpu.VMEM((H * TS, d), F32), pltpu.VMEM((H * TS, LANES), F32)],
    )
    o = pl.pallas_call(
        _sb_sample_kernel,
        grid_spec=grid_spec,
        out_shape=jax.ShapeDtypeStruct((DB, H * TS, dh), F32),
        compiler_params=_cparams(("parallel", "arbitrary")),
        name="sb_sample_attention",
    )(page_table, q_blocks, cache_t, new_t, jnp.asarray(u, BF16), jnp.asarray(fold, F32))
    return o.reshape(DB, H, TS, dh).transpose(0, 2, 1, 3).reshape(DB * TS, d)


NSA_QT = 128
NSA_KT = 512


def _nsa_first_tile(qi, window):
    if not window:
        return 0
    return jnp.maximum(qi * NSA_QT - (WINDOW - 1), 0) // NSA_KT


def _nsa_attn_kernel(*refs, window):
    if window:
        q_ref, k_ref, v_ref, o_ref, m_ref, l_ref, acc_ref = refs
    else:
        q_ref, k_ref, v_ref, sel_ref, e_ref, o_ref, m_ref, l_ref, acc_ref = refs
    qi = pl.program_id(2)
    step = pl.program_id(3)
    kt = _nsa_first_tile(qi, window) + step
    width = NSA_GROUP * HEAD_DIM
    lane_head = lax.broadcasted_iota(jnp.int32, (NSA_QT, width), 1) // HEAD_DIM

    @pl.when(step == 0)
    def _():
        m_ref[...] = jnp.full_like(m_ref, NEG_BIG)
        l_ref[...] = jnp.zeros_like(l_ref)
        acc_ref[...] = jnp.zeros_like(acc_ref)

    @pl.when(kt * NSA_KT <= qi * NSA_QT + NSA_QT - 1)
    def _():
        q = q_ref[...] * jnp.asarray(HEAD_DIM ** -0.5, BF16)
        zero = jnp.zeros_like(q)
        row = lax.broadcasted_iota(jnp.int32, (NSA_QT, LANES), 0)
        col = lax.broadcasted_iota(jnp.int32, (NSA_QT, LANES), 1)
        n_slab = NSA_KT // LANES
        valid = []
        for j in range(n_slab):
            dist = (qi * NSA_QT + row) - (kt * NSA_KT + j * LANES + col)
            if window:
                valid.append(jnp.logical_and(dist >= 0, dist < WINDOW))
            else:
                chosen = jnp.dot(sel_ref[...], e_ref[:, j * LANES:(j + 1) * LANES], preferred_element_type=F32)
                valid.append(jnp.logical_and(chosen > 0.5, dist >= 0))
        k = k_ref[...]
        v = v_ref[...]
        for r in range(NSA_GROUP):
            qr = jnp.where(lane_head == r, q, zero)
            s = lax.dot_general(qr, k, (((1,), (1,)), ((), ())), preferred_element_type=F32)
            slabs = [jnp.where(valid[j], s[:, j * LANES:(j + 1) * LANES], NEG_BIG) for j in range(n_slab)]
            slab_max = functools.reduce(jnp.maximum, slabs)
            m_prev = m_ref[r]
            m_new = jnp.maximum(m_prev, jnp.max(slab_max, axis=-1, keepdims=True))
            alpha = jnp.exp(m_prev - m_new)
            p = [jnp.where(valid[j], jnp.exp(slabs[j] - m_new), 0.0) for j in range(n_slab)]
            p_sum = functools.reduce(jnp.add, p)
            l_ref[r] = alpha * l_ref[r] + jnp.sum(p_sum, axis=-1, keepdims=True)
            m_ref[r] = m_new
            pv = jnp.dot(jnp.concatenate([pj.astype(BF16) for pj in p], axis=1), v, preferred_element_type=F32)
            acc_ref[r] = jnp.concatenate([alpha, alpha], axis=1) * acc_ref[r] + pv

    @pl.when(step == pl.num_programs(3) - 1)
    def _():
        out = jnp.zeros((NSA_QT, width), F32)
        for r in range(NSA_GROUP):
            denom = jnp.maximum(l_ref[r], TINY)
            o_r = acc_ref[r] / jnp.concatenate([denom, denom], axis=1)
            out = jnp.where(lane_head == r, o_r, out)
        o_ref[...] = out


def nsa_prompt_attention(q, k_rep, v_rep, sel, batch, seq):
    window = sel is None
    width = NSA_GROUP * HEAD_DIM
    nq = seq // NSA_QT
    nk = seq // NSA_KT
    first = [max(i * NSA_QT - (WINDOW - 1), 0) // NSA_KT if window else 0 for i in range(nq)]
    steps = max((i * NSA_QT + NSA_QT - 1) // NSA_KT - first[i] + 1 for i in range(nq))

    def key_tile(i, t):
        return jnp.minimum(_nsa_first_tile(i, window) + t, (i * NSA_QT + NSA_QT - 1) // NSA_KT)

    kv_spec = pl.BlockSpec((NSA_KT, width), lambda b, g, i, t: (b * nk + key_tile(i, t), g))
    in_specs = [pl.BlockSpec((NSA_QT, width), lambda b, g, i, t: (b * nq + i, g)), kv_spec, kv_spec]
    args = [q, k_rep, v_rep]
    if not window:
        n_sel = seq // SEL_BLOCK
        expand = (np.arange(n_sel)[:, None] == (np.arange(seq)[None, :] // SEL_BLOCK)).astype(np.float32)
        in_specs += [pl.BlockSpec((NSA_QT, n_sel), lambda b, g, i, t: ((b * NSA_KV_HEADS + g) * nq + i, 0)),
                     pl.BlockSpec((n_sel, NSA_KT), lambda b, g, i, t: (0, key_tile(i, t)))]
        args += [sel, jnp.asarray(expand, BF16)]
    return pl.pallas_call(
        functools.partial(_nsa_attn_kernel, window=window),
        grid=(batch, NSA_KV_HEADS, nq, steps),
        in_specs=in_specs,
        out_specs=pl.BlockSpec((NSA_QT, width), lambda b, g, i, t: (b * nq + i, g)),
        out_shape=jax.ShapeDtypeStruct((batch * seq, NSA_HEADS * HEAD_DIM), F32),
        scratch_shapes=[pltpu.VMEM((NSA_GROUP, NSA_QT, LANES), F32),
                        pltpu.VMEM((NSA_GROUP, NSA_QT, LANES), F32),
                        pltpu.VMEM((NSA_GROUP, NSA_QT, width), F32)],
        compiler_params=_cparams(("parallel", "parallel", "parallel", "arbitrary")),
        name="nsa_window_prompt" if window else "nsa_selected_prompt",
    )(*args)


def head_rmsnorm(h, g, dtype):
    y = h * lax.rsqrt(jnp.mean(h * h, axis=-1, keepdims=True) + EPS)
    return (y * g.astype(jnp.float32)).astype(dtype)


def masked_softmax(s, valid, axes):
    s = jnp.where(valid, s, NEG_BIG)
    m = jnp.max(s, axis=axes, keepdims=True)
    e = jnp.where(valid, jnp.exp(s - m), 0.0)
    return e / jnp.maximum(jnp.sum(e, axis=axes, keepdims=True), TINY)


def split_cols(h, sizes):
    offs = [int(o) for o in np.cumsum(sizes)[:-1]]
    return jnp.split(h, offs, axis=-1)


def _repeat_group(a):
    B, T, G, dh = a.shape
    return jnp.tile(a.astype(BF16)[:, :, :, None, :], (1, 1, 1, NSA_GROUP, 1)).reshape(B * T, G * NSA_GROUP * dh)


def nsa_compress(rows, pos_emb, w):
    B, Tk, G, dh = rows.shape
    blk = rows.reshape(B, Tk // CMP_BLOCK, CMP_BLOCK, G, dh) + pos_emb[None, None, :, None, :]
    blk = blk.transpose(0, 1, 3, 2, 4).reshape(B, Tk // CMP_BLOCK, G, CMP_BLOCK * dh)
    return blk @ w


def nsa_compressed_selected(q, rows, q_pos, cmp_pos, cmp_w):
    B, Tq, H, dh = q.shape
    G, R = NSA_KV_HEADS, NSA_GROUP
    pad = (-rows.shape[1]) % SEL_BLOCK
    rows = jnp.pad(rows, ((0, 0), (0, pad), (0, 0), (0, 0), (0, 0)))
    Tp = rows.shape[1]
    scale = dh ** -0.5
    qg = q.reshape(B, Tq, G, R, dh)
    k_cmp = nsa_compress(rows[:, :, 0], cmp_pos[0], cmp_w[0])
    v_cmp = nsa_compress(rows[:, :, 1], cmp_pos[1], cmp_w[1])
    n_cmp = Tp // CMP_BLOCK
    s = jnp.einsum('btgrd,bngd->bgrtn', qg, k_cmp).astype(jnp.float32) * scale
    cmp_end = (jnp.arange(n_cmp) + 1) * CMP_BLOCK - 1
    p = masked_softmax(s, cmp_end[None, :] <= q_pos[:, None], -1)
    o_cmp = jnp.einsum('bgrtn,bngd->btgrd', p.astype(v_cmp.dtype), v_cmp).reshape(B, Tq, H, dh)
    n_sel = Tp // SEL_BLOCK
    imp = p.sum(axis=2).reshape(B, G, Tq, n_sel, SEL_BLOCK // CMP_BLOCK).sum(axis=-1)
    blk = jnp.arange(n_sel)
    forced = (blk[None, :] == 0) | (blk[None, :] == (q_pos // SEL_BLOCK)[:, None])
    avail = blk[None, :] * SEL_BLOCK <= q_pos[:, None]
    imp = jnp.where(avail, jnp.where(forced, FORCED_SCORE, imp), -1.0)
    _, idx = lax.top_k(imp, min(TOP_N, n_sel))
    n_top = idx.shape[-1]
    if Tq == Tp and Tq % NSA_KT == 0:
        sel = jnp.sum(jax.nn.one_hot(idx, n_sel, dtype=BF16), axis=-2).reshape(B * G * Tq, n_sel)
        o_sel = nsa_prompt_attention(q.astype(BF16).reshape(B * Tq, H * dh), _repeat_group(rows[:, :, 2]),
                                     _repeat_group(rows[:, :, 3]), sel, B, Tq)
        return o_cmp, o_sel.reshape(B, Tq, H, dh)
    if Tq <= SEL_QUERY_BLOCK:
        chosen = jnp.sum(jax.nn.one_hot(idx, n_sel, dtype=jnp.float32), axis=-2) > 0.5
        ok = jnp.repeat(chosen, SEL_BLOCK, axis=-1) & (jnp.arange(Tp)[None, :] <= q_pos[:, None])
        sc = jnp.einsum('bqgrd,bkgd->bgrqk', qg, rows[:, :, 2]).astype(jnp.float32) * scale
        w = masked_softmax(sc, ok[:, :, None], -1)
        o_sel = jnp.einsum('bgrqk,bkgd->bqgrd', w.astype(rows.dtype), rows[:, :, 3])
        return o_cmp, o_sel.reshape(B, Tq, H, dh)
    ks = rows[:, :, 2].reshape(B, n_sel, SEL_BLOCK, G, dh).transpose(0, 3, 1, 2, 4)
    vs = rows[:, :, 3].reshape(B, n_sel, SEL_BLOCK, G, dh).transpose(0, 3, 1, 2, 4)
    qb = SEL_QUERY_BLOCK if Tq % SEL_QUERY_BLOCK == 0 else Tq
    nqb = Tq // qb
    q_blocks = qg.reshape(B, nqb, qb, G, R, dh).swapaxes(0, 1)
    idx_blocks = idx.reshape(B, G, nqb, qb, n_top).transpose(2, 0, 1, 3, 4)
    pos_blocks = q_pos.reshape(nqb, qb)
    b_ix = jnp.arange(B)[:, None, None, None]
    g_ix = jnp.arange(G)[None, :, None, None]

    def sel_block(args):
        qc, ic, pc = args
        kg = ks[b_ix, g_ix, ic]
        vg = vs[b_ix, g_ix, ic]
        sc = jnp.einsum('bqgrd,bgqnkd->bgrqnk', qc, kg).astype(jnp.float32) * scale
        kpos = ic[..., None] * SEL_BLOCK + jnp.arange(SEL_BLOCK)
        ok = (kpos <= pc[None, None, :, None, None])[:, :, None]
        w = masked_softmax(sc, ok, (-2, -1))
        return jnp.einsum('bgrqnk,bgqnkd->bqgrd', w.astype(vg.dtype), vg)

    o_sel = lax.map(sel_block, (q_blocks, idx_blocks, pos_blocks))
    o_sel = o_sel.swapaxes(0, 1).reshape(B, Tq, H, dh)
    return o_cmp, o_sel


def window_core(qg, k, v, q_pos, k_pos):
    s = jnp.einsum('btgrd,bkgd->bgrtk', qg, k).astype(jnp.float32) * (qg.shape[-1] ** -0.5)
    dist = q_pos[:, None] - k_pos[None, :]
    ok = (dist >= 0) & (dist < WINDOW) & (k_pos[None, :] >= 0)
    w = masked_softmax(s, ok, -1)
    return jnp.einsum('bgrtk,bkgd->btgrd', w.astype(v.dtype), v)


def window_banded(qg, k, v):
    B, T = qg.shape[:2]
    qb = QUERY_BLOCK if T % QUERY_BLOCK == 0 else T
    nqb = T // qb
    span = WINDOW + qb
    kidx = jnp.arange(nqb)[:, None] * qb + jnp.arange(span)[None, :]
    pad = ((0, 0), (WINDOW, 0), (0, 0), (0, 0))
    kb = jnp.pad(k, pad)[:, kidx]
    vb = jnp.pad(v, pad)[:, kidx]
    q_pos = jnp.arange(T).reshape(nqb, qb)
    k_pos = kidx - WINDOW
    q_blocks = qg.reshape(B, nqb, qb, *qg.shape[2:])
    o = jax.vmap(window_core, in_axes=(1, 1, 1, 0, 0), out_axes=1)(q_blocks, kb, vb, q_pos, k_pos)
    return o.reshape(qg.shape)


def mlstm_chunkwise(q, k, v, i_pre, f_pre, C0, n0, m0):
    B, T, H, d = q.shape
    L = MLSTM_CHUNK if T % MLSTM_CHUNK == 0 else T
    nc = T // L
    f32 = jnp.float32
    q, k, v = q.astype(f32), k.astype(f32) * (d ** -0.5), v.astype(f32)
    log_f = jax.nn.log_sigmoid(f_pre.astype(f32))
    i_pre = i_pre.astype(f32)
    chunks = lambda a: a.reshape(B, nc, L, *a.shape[2:]).swapaxes(0, 1)
    causal = jnp.tril(jnp.ones((L, L), dtype=bool))

    def step(carry, xs):
        C, n, m = carry
        qc, kc, vc, ic, fc = xs
        b = jnp.cumsum(fc, axis=1).swapaxes(1, 2)
        it = ic.swapaxes(1, 2)
        log_d = jnp.where(causal, b[..., :, None] - b[..., None, :] + it[..., None, :], -jnp.inf)
        m_inter = b + m[..., None]
        m_t = jnp.maximum(m_inter, jnp.max(log_d, axis=-1))
        w = jnp.einsum('blhd,bshd->bhls', qc, kc) * jnp.exp(log_d - m_t[..., None])
        carry_w = jnp.exp(m_inter - m_t)
        num = jnp.einsum('bhls,bshd->blhd', w, vc) + jnp.einsum('blhd,bhde->blhe', qc, C) * carry_w.swapaxes(1, 2)[..., None]
        den = jnp.sum(w, axis=-1) + jnp.einsum('blhd,bhd->bhl', qc, n) * carry_w
        den = jnp.maximum(jnp.abs(den), jnp.exp(-m_t))
        h = num / den.swapaxes(1, 2)[..., None]
        m_new = m_t[..., -1]
        w_end = jnp.exp(b[..., -1:] - b + it - m_new[..., None])
        decay = jnp.exp(b[..., -1] + m - m_new)
        C_new = decay[..., None, None] * C + jnp.einsum('bhs,bshd,bshe->bhde', w_end, kc, vc)
        n_new = decay[..., None] * n + jnp.einsum('bhs,bshd->bhd', w_end, kc)
        return (C_new, n_new, m_new), h

    (C, n, m), hs = lax.scan(step, (C0.astype(f32), n0.astype(f32), m0.astype(f32)),
                             tuple(chunks(a) for a in (q, k, v, i_pre, log_f)))
    return hs.swapaxes(0, 1).reshape(B, T, H, d), C, n, m


def nsa_mlstm_core(proj, start, past_rows, win_buf, C0, n0, m0, nsa_cmp_pos, nsa_cmp_w,
                   mlstm_b_i, mlstm_b_f, mlstm_norm):
    B, T, _ = proj.shape
    (q_a, k_c, v_c, k_s, v_s, k_w, v_w, g_a, q_m, k_m, v_m, o_m, i_m, f_m) = split_cols(proj, IN0_SIZES)
    kvh = lambda a: a.reshape(B, T, NSA_KV_HEADS, HEAD_DIM)
    mh = lambda a: a.reshape(B, T, MLSTM_HEADS, MLSTM_DIM)
    q_a = q_a.reshape(B, T, NSA_HEADS, HEAD_DIM)
    q_pos = start + jnp.arange(T)
    new_rows = jnp.stack([kvh(k_c), kvh(v_c), kvh(k_s), kvh(v_s)], axis=2)
    rows_full = jnp.concatenate([past_rows.astype(new_rows.dtype), new_rows], axis=1)
    o_cmp, o_sel = nsa_compressed_selected(q_a, rows_full, q_pos, nsa_cmp_pos, nsa_cmp_w)
    win_rows = jnp.stack([kvh(k_w), kvh(v_w)], axis=2)
    qg = q_a.reshape(B, T, NSA_KV_HEADS, NSA_GROUP, HEAD_DIM)
    if win_buf is None:
        if T % NSA_KT == 0:
            o_win = nsa_prompt_attention(q_a.astype(BF16).reshape(B * T, NSA_HEADS * HEAD_DIM),
                                         _repeat_group(win_rows[:, :, 0]), _repeat_group(win_rows[:, :, 1]), None, B, T)
        else:
            o_win = window_banded(qg, win_rows[:, :, 0], win_rows[:, :, 1])
        new_win = win_rows[:, T - min(WINDOW, T):]
    else:
        n_buf = win_buf.shape[1]
        buf = jnp.concatenate([win_buf.astype(win_rows.dtype), win_rows], axis=1)
        k_pos = start - n_buf + jnp.arange(n_buf + T)
        o_win = window_core(qg, buf[:, :, 0], buf[:, :, 1], q_pos, k_pos)
        new_win = buf[:, T:]
    o_win = o_win.reshape(B, T, NSA_HEADS, HEAD_DIM)
    gates = jax.nn.sigmoid(g_a.reshape(B, T, NSA_HEADS, 3))
    o_a = gates[..., 0:1] * o_cmp + gates[..., 1:2] * o_sel + gates[..., 2:3] * o_win
    h_m, C, n, m = mlstm_chunkwise(mh(q_m), mh(k_m), mh(v_m), i_m + mlstm_b_i, f_m + mlstm_b_f, C0, n0, m0)
    h_m = head_rmsnorm(h_m, mlstm_norm, proj.dtype) * jax.nn.sigmoid(mh(o_m))
    mixed = jnp.concatenate([o_a.reshape(B, T, -1), h_m.reshape(B, T, -1)], axis=-1)
    return mixed, new_rows, new_win, C, n, m


def kernel(x_prompt, x_sample, cache_nsa_kv, state_nsa_win, state_mlstm_C, state_mlstm_n, state_mlstm_m, cache_sb_kv, page_table, norm_mix0, w_in0, nsa_cmp_pos, nsa_cmp_w, mlstm_b_i, mlstm_b_f, mlstm_norm, w_out0, norm_ffn0, w_gate0, w_up0, w_down0, norm_mix1, w_qkv1, w_out1, norm_ffn1, w_router1, w_gate1, w_up1, w_down1, norm_final):
    B, T, D = x_prompt.shape
    DB, TS, _ = x_sample.shape
    n_p = B * T
    past_len = page_table.shape[1] * cache_nsa_kv.shape[1]
    x = jnp.concatenate([x_prompt.reshape(n_p, D), x_sample.reshape(DB * TS, D)], axis=0)

    hn = rmsnorm_tokens(x, norm_mix0, BF16)
    in0_pad = (-IN0_COLS) % (9 * LANES)
    w_in = jnp.pad(w_in0, ((0, 0), (0, in0_pad))).astype(BF16)
    proj = matmul(hn, w_in, tn=(IN0_COLS + in0_pad) // 3)[:, :IN0_COLS]
    proj_p = proj[:n_p].reshape(B, T, IN0_COLS)
    proj_s = proj[n_p:].reshape(DB, TS, IN0_COLS)
    empty = jnp.zeros((B, 0, 4, NSA_KV_HEADS, HEAD_DIM), F32)
    c0 = jnp.zeros((B, MLSTM_HEADS, MLSTM_DIM, MLSTM_DIM), F32)
    n0 = jnp.zeros((B, MLSTM_HEADS, MLSTM_DIM), F32)
    m0 = jnp.zeros((B, MLSTM_HEADS), F32)
    mixed_p, nsa_rows_p, nsa_win_p, mC_p, mn_p, mm_p = nsa_mlstm_core(
        proj_p, 0, empty, None, c0, n0, m0, nsa_cmp_pos, nsa_cmp_w, mlstm_b_i, mlstm_b_f, mlstm_norm)
    past_rows = cache_nsa_kv[page_table].reshape(DB, past_len, 4, NSA_KV_HEADS, HEAD_DIM)
    mixed_s, nsa_rows_s, nsa_win_s, mC_s, mn_s, mm_s = nsa_mlstm_core(
        proj_s, past_len, past_rows, state_nsa_win, state_mlstm_C, state_mlstm_n, state_mlstm_m,
        nsa_cmp_pos, nsa_cmp_w, mlstm_b_i, mlstm_b_f, mlstm_norm)
    mixed = jnp.concatenate([mixed_p.reshape(n_p, D), mixed_s.reshape(DB * TS, D)], axis=0)
    x = matmul(mixed.astype(BF16), w_out0.astype(BF16), res=x)
    x = dense_ffn(rmsnorm_tokens(x, norm_ffn0, BF16), x, w_gate0, w_up0, w_down0)

    hn = rmsnorm_tokens(x, norm_mix1, BF16)
    qkv, qkv_b = matmul(hn, w_qkv1.astype(BF16), also_bf16=True)
    sb_rows_p = qkv[:n_p].reshape(B, T, 3, SB_HEADS, HEAD_DIM)[:, :, 1:]
    qkv_s = qkv[n_p:].reshape(DB, TS, 3, SB_HEADS, HEAD_DIM)
    o_p = sb_prompt_attention(qkv_b, B, T)
    o_s = sb_sample_attention(qkv_s, cache_sb_kv, page_table)
    sb_rows_s = qkv_s[:, :, 1:]
    o = jnp.concatenate([o_p, o_s.astype(BF16)], axis=0)
    x = matmul(o, w_out1.astype(BF16), res=x)
    hn, logits = rmsnorm_router(x, norm_ffn1, w_router1)
    x = moe_ffn(hn, logits[:, :N_EXPERTS], x, w_gate1, w_up1, w_down1)

    y = rmsnorm_tokens(x, norm_final, F32)
    y_prompt = y[:n_p].reshape(B, T, D)
    y_sample = y[n_p:].reshape(DB, TS, D)
    return (y_prompt, y_sample, nsa_rows_p, nsa_rows_s, nsa_win_p, nsa_win_s, mC_p, mn_p, mm_p,
            mC_s, mn_s, mm_s, sb_rows_p, sb_rows_s)
```

```python
import functools

import numpy as np
import jax
import jax.numpy as jnp
from jax import lax
from jax.experimental import pallas as pl
from jax.experimental.pallas import tpu as pltpu

D_MODEL = 1024
HEAD_DIM = 64
NSA_HEADS = 8
NSA_KV_HEADS = 2
NSA_GROUP = NSA_HEADS // NSA_KV_HEADS
CMP_BLOCK = 32
SEL_BLOCK = 64
TOP_N = 16
WINDOW = 512
SEL_QUERY_BLOCK = 64
FORCED_SCORE = 1e4
MLSTM_HEADS = 4
MLSTM_DIM = 128
MLSTM_CHUNK = 64
SB_HEADS = D_MODEL // HEAD_DIM
QUERY_BLOCK = 128
N_EXPERTS = 8
TOP_K = 2
EPS = 1e-6
NEG_BIG = -1e30
TINY = 1e-30

IN0_SIZES = (NSA_HEADS * HEAD_DIM,) + (NSA_KV_HEADS * HEAD_DIM,) * 6 + (3 * NSA_HEADS,) + (MLSTM_HEADS * MLSTM_DIM,) * 4 + (MLSTM_HEADS, MLSTM_HEADS)
IN0_COLS = sum(IN0_SIZES)

LANES = 128
TOKEN_TILE = 512
VMEM_LIMIT = 48 * 1024 * 1024
BF16 = jnp.bfloat16
F32 = jnp.float32


def _cparams(sem):
    return pltpu.CompilerParams(dimension_semantics=sem, vmem_limit_bytes=VMEM_LIMIT)


def _rmsnorm_kernel(x_ref, g_ref, o_ref):
    x = x_ref[...]
    y = x * lax.rsqrt(jnp.mean(x * x, axis=-1, keepdims=True) + EPS)
    o_ref[...] = (y * g_ref[...]).astype(o_ref.dtype)


def rmsnorm_tokens(x, g, out_dtype):
    n, d = x.shape
    return pl.pallas_call(
        _rmsnorm_kernel,
        grid=(n // TOKEN_TILE,),
        in_specs=[pl.BlockSpec((TOKEN_TILE, d), lambda i: (i, 0)),
                  pl.BlockSpec((1, d), lambda i: (0, 0))],
        out_specs=pl.BlockSpec((TOKEN_TILE, d), lambda i: (i, 0)),
        out_shape=jax.ShapeDtypeStruct((n, d), out_dtype),
        compiler_params=_cparams(("parallel",)),
        name="rmsnorm",
    )(x, g.reshape(1, d))


def _rmsnorm_router_kernel(x_ref, g_ref, wr_ref, o_ref, logit_ref):
    x = x_ref[...]
    y = x * lax.rsqrt(jnp.mean(x * x, axis=-1, keepdims=True) + EPS) * g_ref[...]
    o_ref[...] = y.astype(o_ref.dtype)
    logit_ref[...] = jnp.dot(y, wr_ref[...], precision=lax.Precision.HIGHEST,
                             preferred_element_type=F32)


def rmsnorm_router(x, g, w_router):
    n, d = x.shape
    wr = jnp.pad(w_router, ((0, 0), (0, LANES - w_router.shape[1])))
    return pl.pallas_call(
        _rmsnorm_router_kernel,
        grid=(n // TOKEN_TILE,),
        in_specs=[pl.BlockSpec((TOKEN_TILE, d), lambda i: (i, 0)),
                  pl.BlockSpec((1, d), lambda i: (0, 0)),
                  pl.BlockSpec((d, LANES), lambda i: (0, 0))],
        out_specs=[pl.BlockSpec((TOKEN_TILE, d), lambda i: (i, 0)),
                   pl.BlockSpec((TOKEN_TILE, LANES), lambda i: (i, 0))],
        out_shape=[jax.ShapeDtypeStruct((n, d), BF16),
                   jax.ShapeDtypeStruct((n, LANES), F32)],
        compiler_params=_cparams(("parallel",)),
        name="rmsnorm_router",
    )(x, g.reshape(1, d), wr)


def _matmul_kernel(a_ref, w_ref, o_ref):
    o_ref[...] = jnp.dot(a_ref[...], w_ref[...], preferred_element_type=F32)


def _matmul_dual_kernel(a_ref, w_ref, o_ref, ob_ref):
    y = jnp.dot(a_ref[...], w_ref[...], preferred_element_type=F32)
    o_ref[...] = y
    ob_ref[...] = y.astype(BF16)


def _matmul_res_kernel(a_ref, w_ref, r_ref, o_ref):
    o_ref[...] = r_ref[...] + jnp.dot(a_ref[...], w_ref[...], preferred_element_type=F32)


def matmul(a, w, res=None, tn=1024, also_bf16=False):
    n, k = a.shape
    m = w.shape[1]
    tn = min(tn, m)
    assert n % TOKEN_TILE == 0 and m % tn == 0
    assert not (also_bf16 and res is not None)
    in_specs = [pl.BlockSpec((TOKEN_TILE, k), lambda i, j: (i, 0)),
                pl.BlockSpec((k, tn), lambda i, j: (0, j))]
    args = [a, w]
    body = _matmul_kernel
    out_spec = pl.BlockSpec((TOKEN_TILE, tn), lambda i, j: (i, j))
    out_specs = out_spec
    out_shape = jax.ShapeDtypeStruct((n, m), F32)
    if res is not None:
        in_specs.append(pl.BlockSpec((TOKEN_TILE, tn), lambda i, j: (i, j)))
        args.append(res)
        body = _matmul_res_kernel
    if also_bf16:
        body = _matmul_dual_kernel
        out_specs = [out_spec, out_spec]
        out_shape = [out_shape, jax.ShapeDtypeStruct((n, m), BF16)]
    return pl.pallas_call(
        body,
        grid=(n // TOKEN_TILE, m // tn),
        in_specs=in_specs,
        out_specs=out_specs,
        out_shape=out_shape,
        compiler_params=_cparams(("parallel", "parallel")),
        name="matmul",
    )(*args)


def _swiglu_kernel(te_ref, nv_ref, x_ref, wg_ref, wu_ref, wd_ref, cw_ref, *rest, has_res):
    if has_res:
        r_ref, o_ref, acc_ref = rest
    else:
        o_ref, acc_ref = rest
    i = pl.program_id(0)
    j = pl.program_id(1)

    @pl.when(j == 0)
    def _():
        acc_ref[...] = jnp.zeros_like(acc_ref)

    @pl.when(i < nv_ref[0])
    def _():
        x = x_ref[...]
        g = jnp.dot(x, wg_ref[0], preferred_element_type=F32)
        u = jnp.dot(x, wu_ref[0], preferred_element_type=F32)
        h = (g * jax.nn.sigmoid(g)) * u
        acc_ref[...] += jnp.dot(h.astype(BF16), wd_ref[0], preferred_element_type=F32)

    @pl.when(j == pl.num_programs(1) - 1)
    def _():
        y = acc_ref[...] * cw_ref[...]
        if has_res:
            y = y + r_ref[...]
        o_ref[...] = y


def grouped_swiglu(x, wg, wu, wd, tile_expert, n_valid, cw, res, tf):
    r, d = x.shape
    f = wg.shape[2]
    assert r % TOKEN_TILE == 0 and f % tf == 0
    has_res = res is not None
    in_specs = [
        pl.BlockSpec((TOKEN_TILE, d), lambda i, j, te, nv: (i, 0)),
        pl.BlockSpec((1, d, tf), lambda i, j, te, nv: (te[i], 0, j)),
        pl.BlockSpec((1, d, tf), lambda i, j, te, nv: (te[i], 0, j)),
        pl.BlockSpec((1, tf, d), lambda i, j, te, nv: (te[i], j, 0)),
        pl.BlockSpec((TOKEN_TILE, 1), lambda i, j, te, nv: (i, 0)),
    ]
    args = [x, wg, wu, wd, cw]
    if has_res:
        in_specs.append(pl.BlockSpec((TOKEN_TILE, d), lambda i, j, te, nv: (i, 0)))
        args.append(res)
    grid_spec = pltpu.PrefetchScalarGridSpec(
        num_scalar_prefetch=2,
        grid=(r // TOKEN_TILE, f // tf),
        in_specs=in_specs,
        out_specs=pl.BlockSpec((TOKEN_TILE, d), lambda i, j, te, nv: (i, 0)),
        scratch_shapes=[pltpu.VMEM((TOKEN_TILE, d), F32)],
    )
    return pl.pallas_call(
        functools.partial(_swiglu_kernel, has_res=has_res),
        grid_spec=grid_spec,
        out_shape=jax.ShapeDtypeStruct((r, d), F32),
        compiler_params=_cparams(("parallel", "arbitrary")),
        name="grouped_swiglu",
    )(tile_expert, n_valid, *args)


def dense_ffn(xn, res, wg, wu, wd):
    n = xn.shape[0]
    tiles = n // TOKEN_TILE
    return grouped_swiglu(xn, wg[None].astype(BF16), wu[None].astype(BF16), wd[None].astype(BF16),
                          jnp.zeros((tiles,), jnp.int32), jnp.full((1,), tiles, jnp.int32),
                          jnp.ones((n, 1), F32), res, tf=256)


def moe_ffn(xn, logits, res, wg, wu, wd):
    n, d = xn.shape
    probs = jax.nn.softmax(logits, axis=-1)
    top_w, top_i = lax.top_k(probs, TOP_K)
    top_w = top_w / jnp.sum(top_w, axis=-1, keepdims=True)
    flat_e = top_i.reshape(-1)
    order = jnp.argsort(flat_e, stable=True)
    sorted_e = flat_e[order]
    counts = jnp.sum(jax.nn.one_hot(flat_e, N_EXPERTS, dtype=jnp.int32), axis=0)
    padded = ((counts + TOKEN_TILE - 1) // TOKEN_TILE) * TOKEN_TILE
    ends_p = jnp.cumsum(padded)
    starts_p = ends_p - padded
    starts = jnp.cumsum(counts) - counts
    s_ix = jnp.arange(TOP_K * n, dtype=jnp.int32)
    dest = starts_p[sorted_e] + (s_ix - starts[sorted_e])
    rows = TOP_K * n + N_EXPERTS * TOKEN_TILE
    src_tok = jnp.zeros((rows,), jnp.int32).at[dest].set(order // TOP_K)
    cw = jnp.zeros((rows,), F32).at[dest].set(top_w.reshape(-1)[order])
    pos = jnp.zeros((TOP_K * n,), jnp.int32).at[order].set(dest)
    tiles = rows // TOKEN_TILE
    tile_start = jnp.arange(tiles, dtype=jnp.int32) * TOKEN_TILE
    tile_expert = jnp.minimum(jnp.searchsorted(ends_p, tile_start, side="right"), N_EXPERTS - 1).astype(jnp.int32)
    n_valid = (ends_p[-1] // TOKEN_TILE).astype(jnp.int32).reshape(1)
    x_sorted = xn[src_tok]
    y_sorted = grouped_swiglu(x_sorted, wg.astype(BF16), wu.astype(BF16), wd.astype(BF16),
                              tile_expert, n_valid, cw.reshape(rows, 1), None, tf=512)
    pos = pos.reshape(n, TOP_K)
    return res + y_sorted[pos[:, 0]] + y_sorted[pos[:, 1]]


SB_QT = 128
SB_KT = 128
SB_UNROLL = 8


def _sb_prompt_kernel(q_ref, k_ref, v_ref, u_ref, o_ref, acc_ref, r_ref):
    qi = pl.program_id(2)
    half = LANES // 2
    lane = lax.broadcasted_iota(jnp.int32, (SB_QT, LANES), 1)
    q = q_ref[...] * jnp.asarray(HEAD_DIM ** -0.5, BF16)
    zero = jnp.zeros_like(q)
    q_heads = (jnp.where(lane < half, q, zero), jnp.where(lane >= half, q, zero))
    u = u_ref[...]
    acc_ref[...] = jnp.zeros_like(acc_ref)
    r_ref[...] = jnp.zeros_like(r_ref)
    row = lax.broadcasted_iota(jnp.int32, (SB_QT, SB_KT), 0)
    col = lax.broadcasted_iota(jnp.int32, (SB_QT, SB_KT), 1)

    def tiles(kts, masked):
        parts = []
        for kt in kts:
            start = pl.multiple_of(kt * SB_KT, SB_KT)
            k = k_ref[pl.ds(start, SB_KT), :]
            v = v_ref[pl.ds(start, SB_KT), :]
            before = ((kt * SB_KT + col) < (qi * SB_QT + row)) if masked else None
            for hh in range(2):
                z = lax.dot_general(q_heads[hh], k, (((1,), (1,)), ((), ())), preferred_element_type=F32)
                l = jnp.log(1.0 + jnp.exp(-jnp.abs(z)))
                log_keep = -(jnp.maximum(z, 0.0) + l)
                if masked:
                    log_keep = jnp.where(before, log_keep, 0.0)
                hi = log_keep.astype(BF16)
                lo = (log_keep - hi.astype(F32)).astype(BF16)
                cs = jnp.dot(jnp.concatenate([hi, lo], axis=1), u, preferred_element_type=F32)
                parts.append((hh, z, cs, before, v))
        r = [r_ref[0], r_ref[1]]
        total = None
        for t in range(len(kts)):
            pv = []
            for hh, z, cs, before, v in parts[2 * t:2 * t + 2]:
                a = jnp.exp(z + cs[:, :SB_KT] + r[hh])
                if masked:
                    a = jnp.where(before, a, 0.0)
                r[hh] = r[hh] + cs[:, SB_KT:]
                pv.append(jnp.dot(a.astype(BF16), v, preferred_element_type=F32))
            both = jnp.where(lane < half, pv[0], pv[1])
            total = both if total is None else total + both
        r_ref[0] = r[0]
        r_ref[1] = r[1]
        acc_ref[...] += total

    n_diag = SB_QT // SB_KT
    n_full = qi * n_diag
    tiles([n_full + n_diag - 1 - d for d in range(n_diag)], True)
    rem = n_full % SB_UNROLL

    def body_group(i, carry):
        first = n_full - 1 - i * SB_UNROLL
        tiles([first - d for d in range(SB_UNROLL)], False)
        return carry

    lax.fori_loop(0, n_full // SB_UNROLL, body_group, 0)

    size = SB_UNROLL // 2
    while size >= 1:
        @pl.when((rem // size) % 2 == 1)
        def _(size=size):
            first = rem % (2 * size) - 1
            tiles([first - d for d in range(size)], False)
        size //= 2
    o_ref[...] = acc_ref[...].astype(o_ref.dtype)


def sb_prompt_attention(qkv, batch, seq):
    d = SB_HEADS * HEAD_DIM
    n_pair = d // LANES
    nq = seq // SB_QT
    jj = np.arange(SB_KT)
    u = np.concatenate([(jj[:, None] >= jj[None, :]).astype(np.float32), np.ones((SB_KT, SB_KT), np.float32)], axis=1)
    u = np.concatenate([u, u], axis=0)
    return pl.pallas_call(
        _sb_prompt_kernel,
        grid=(batch, n_pair, nq),
        in_specs=[pl.BlockSpec((SB_QT, LANES), lambda b, p, i: (b * nq + i, p)),
                  pl.BlockSpec((seq, LANES), lambda b, p, i: (b, n_pair + p)),
                  pl.BlockSpec((seq, LANES), lambda b, p, i: (b, 2 * n_pair + p)),
                  pl.BlockSpec((2 * SB_KT, 2 * SB_KT), lambda b, p, i: (0, 0))],
        out_specs=pl.BlockSpec((SB_QT, LANES), lambda b, p, i: (b * nq + i, p)),
        out_shape=jax.ShapeDtypeStruct((batch * seq, d), BF16),
        scratch_shapes=[pltpu.VMEM((SB_QT, LANES), F32), pltpu.VMEM((2, SB_QT, SB_KT), F32)],
        compiler_params=_cparams(("parallel", "parallel", "arbitrary")),
        name="sb_prompt_attention",
    )(qkv, qkv, qkv, jnp.asarray(u, BF16))


def _sb_sample_kernel(pt_ref, q_ref, cache_ref, new_ref, u_ref, fold_ref, o_ref, acc_ref, r_ref):
    s = pl.program_id(1)
    rows, cols = acc_ref.shape
    n_new = rows // SB_HEADS

    @pl.when(s == 0)
    def _():
        acc_ref[...] = jnp.zeros_like(acc_ref)
        r_ref[...] = jnp.zeros_like(r_ref)

    def accumulate(kt, vt, valid):
        z = jnp.dot(q_ref[0], kt, preferred_element_type=F32)
        l = jnp.log(1.0 + jnp.exp(-jnp.abs(z)))
        log_keep = -(jnp.maximum(z, 0.0) + l)
        if valid is not None:
            log_keep = jnp.where(valid, log_keep, 0.0)
        hi = log_keep.astype(BF16)
        lo = (log_keep - hi.astype(F32)).astype(BF16)
        cs = jnp.dot(jnp.concatenate([hi, lo], axis=1), u_ref[...], preferred_element_type=F32)
        a = jnp.exp(z + cs[:, :LANES] + r_ref[...])
        if valid is not None:
            a = jnp.where(valid, a, 0.0)
        r_ref[...] += cs[:, LANES:]
        acc_ref[...] += lax.dot_general(a.astype(BF16), vt, (((1,), (1,)), ((), ())), preferred_element_type=F32)

    @pl.when(s == 0)
    def _():
        row = lax.broadcasted_iota(jnp.int32, (rows, LANES), 0)
        col = lax.broadcasted_iota(jnp.int32, (rows, LANES), 1)
        accumulate(new_ref[0, 0], new_ref[0, 1], col < row % n_new)

    @pl.when(s > 0)
    def _():
        kt = cache_ref[0, 0].reshape(cols, LANES).astype(BF16)
        vt = cache_ref[0, 1].reshape(cols, LANES).astype(BF16)
        accumulate(kt, vt, None)

    @pl.when(s == pl.num_programs(1) - 1)
    def _():
        row_head = lax.broadcasted_iota(jnp.int32, (rows, cols), 0) // n_new
        col_head = lax.broadcasted_iota(jnp.int32, (rows, cols), 1) // HEAD_DIM
        own = jnp.where(row_head == col_head, acc_ref[...], 0.0)
        o_ref[0] = jnp.dot(own, fold_ref[...], precision=lax.Precision.HIGHEST, preferred_element_type=F32)


def sb_sample_attention(qkv_s, cache_sb_kv, page_table):
    DB, TS, _, H, dh = qkv_s.shape
    page = cache_sb_kv.shape[1]
    n_pages = page_table.shape[1]
    assert page == LANES and TS <= LANES
    d = H * dh
    q = qkv_s[:, :, 0].astype(BF16) * jnp.asarray(dh ** -0.5, BF16)
    q_blocks = jnp.einsum('bthd,hg->bhtgd', q, jnp.eye(H, dtype=BF16)).reshape(DB, H * TS, d)
    new_t = jnp.transpose(qkv_s[:, :, 1:], (0, 2, 3, 4, 1)).reshape(DB, 2, d, TS).astype(BF16)
    new_t = jnp.pad(new_t, ((0, 0), (0, 0), (0, 0), (0, LANES - TS)))
    cache_t = jnp.transpose(cache_sb_kv, (0, 2, 3, 4, 1))
    jj = np.arange(LANES)
    u = np.concatenate([(jj[:, None] >= jj[None, :]).astype(np.float32), np.ones((LANES, LANES), np.float32)], axis=1)
    u = np.concatenate([u, u], axis=0)
    fold = (np.arange(d)[:, None] % dh == np.arange(dh)[None, :]).astype(np.float32)

    def page_map(b, s, pt):
        return (pt[b, n_pages - 1 - jnp.maximum(s - 1, 0)], 0, 0, 0, 0)

    grid_spec = pltpu.PrefetchScalarGridSpec(
        num_scalar_prefetch=1,
        grid=(DB, n_pages + 1),
        in_specs=[pl.BlockSpec((1, H * TS, d), lambda b, s, pt: (b, 0, 0)),
                  pl.BlockSpec((1, 2, H, dh, page), page_map),
                  pl.BlockSpec((1, 2, d, LANES), lambda b, s, pt: (b, 0, 0, 0)),
                  pl.BlockSpec((2 * LANES, 2 * LANES), lambda b, s, pt: (0, 0)),
                  pl.BlockSpec((d, dh), lambda b, s, pt: (0, 0))],
        out_specs=pl.BlockSpec((1, H * TS, dh), lambda b, s, pt: (b, 0, 0)),
        scratch_shapes=[pltpu.VMEM((H * TS, d), F32), pltpu.VMEM((H * TS, LANES), F32)],
    )
    o = pl.pallas_call(
        _sb_sample_kernel,
        grid_spec=grid_spec,
        out_shape=jax.ShapeDtypeStruct((DB, H * TS, dh), F32),
        compiler_params=_cparams(("parallel", "arbitrary")),
        name="sb_sample_attention",
    )(page_table, q_blocks, cache_t, new_t, jnp.asarray(u, BF16), jnp.asarray(fold, F32))
    return o.reshape(DB, H, TS, dh).transpose(0, 2, 1, 3).reshape(DB * TS, d)


NSA_QT = 128
NSA_KT = 512


def _nsa_first_tile(qi, window):
    if not window:
        return 0
    return jnp.maximum(qi * NSA_QT - (WINDOW - 1), 0) // NSA_KT


def _nsa_attn_kernel(*refs, window):
    if window:
        q_ref, k_ref, v_ref, o_ref, m_ref, l_ref, acc_ref = refs
    else:
        q_ref, k_ref, v_ref, sel_ref, e_ref, o_ref, m_ref, l_ref, acc_ref = refs
    qi = pl.program_id(2)
    step = pl.program_id(3)
    kt = _nsa_first_tile(qi, window) + step
    width = NSA_GROUP * HEAD_DIM
    lane_head = lax.broadcasted_iota(jnp.int32, (NSA_QT, width), 1) // HEAD_DIM

    @pl.when(step == 0)
    def _():
        m_ref[...] = jnp.full_like(m_ref, NEG_BIG)
        l_ref[...] = jnp.zeros_like(l_ref)
        acc_ref[...] = jnp.zeros_like(acc_ref)

    @pl.when(kt * NSA_KT <= qi * NSA_QT + NSA_QT - 1)
    def _():
        q = q_ref[...] * jnp.asarray(HEAD_DIM ** -0.5, BF16)
        zero = jnp.zeros_like(q)
        row = lax.broadcasted_iota(jnp.int32, (NSA_QT, LANES), 0)
        col = lax.broadcasted_iota(jnp.int32, (NSA_QT, LANES), 1)
        n_slab = NSA_KT // LANES
        valid = []
        for j in range(n_slab):
            dist = (qi * NSA_QT + row) - (kt * NSA_KT + j * LANES + col)
            if window:
                valid.append(jnp.logical_and(dist >= 0, dist < WINDOW))
            else:
                chosen = jnp.dot(sel_ref[...], e_ref[:, j * LANES:(j + 1) * LANES], preferred_element_type=F32)
                valid.append(jnp.logical_and(chosen > 0.5, dist >= 0))
        k = k_ref[...]
        v = v_ref[...]
        for r in range(NSA_GROUP):
            qr = jnp.where(lane_head == r, q, zero)
            s = lax.dot_general(qr, k, (((1,), (1,)), ((), ())), preferred_element_type=F32)
            slabs = [jnp.where(valid[j], s[:, j * LANES:(j + 1) * LANES], NEG_BIG) for j in range(n_slab)]
            slab_max = functools.reduce(jnp.maximum, slabs)
            m_prev = m_ref[r]
            m_new = jnp.maximum(m_prev, jnp.max(slab_max, axis=-1, keepdims=True))
            alpha = jnp.exp(m_prev - m_new)
            p = [jnp.where(valid[j], jnp.exp(slabs[j] - m_new), 0.0) for j in range(n_slab)]
            p_sum = functools.reduce(jnp.add, p)
            l_ref[r] = alpha * l_ref[r] + jnp.sum(p_sum, axis=-1, keepdims=True)
            m_ref[r] = m_new
            pv = jnp.dot(jnp.concatenate([pj.astype(BF16) for pj in p], axis=1), v, preferred_element_type=F32)
            acc_ref[r] = jnp.concatenate([alpha, alpha], axis=1) * acc_ref[r] + pv

    @pl.when(step == pl.num_programs(3) - 1)
    def _():
        out = jnp.zeros((NSA_QT, width), F32)
        for r in range(NSA_GROUP):
            denom = jnp.maximum(l_ref[r], TINY)
            o_r = acc_ref[r] / jnp.concatenate([denom, denom], axis=1)
            out = jnp.where(lane_head == r, o_r, out)
        o_ref[...] = out


def nsa_prompt_attention(q, k_rep, v_rep, sel, batch, seq):
    window = sel is None
    width = NSA_GROUP * HEAD_DIM
    nq = seq // NSA_QT
    nk = seq // NSA_KT
    first = [max(i * NSA_QT - (WINDOW - 1), 0) // NSA_KT if window else 0 for i in range(nq)]
    steps = max((i * NSA_QT + NSA_QT - 1) // NSA_KT - first[i] + 1 for i in range(nq))

    def key_tile(i, t):
        return jnp.minimum(_nsa_first_tile(i, window) + t, (i * NSA_QT + NSA_QT - 1) // NSA_KT)

    kv_spec = pl.BlockSpec((NSA_KT, width), lambda b, g, i, t: (b * nk + key_tile(i, t), g))
    in_specs = [pl.BlockSpec((NSA_QT, width), lambda b, g, i, t: (b * nq + i, g)), kv_spec, kv_spec]
    args = [q, k_rep, v_rep]
    if not window:
        n_sel = seq // SEL_BLOCK
        expand = (np.arange(n_sel)[:, None] == (np.arange(seq)[None, :] // SEL_BLOCK)).astype(np.float32)
        in_specs += [pl.BlockSpec((NSA_QT, n_sel), lambda b, g, i, t: ((b * NSA_KV_HEADS + g) * nq + i, 0)),
                     pl.BlockSpec((n_sel, NSA_KT), lambda b, g, i, t: (0, key_tile(i, t)))]
        args += [sel, jnp.asarray(expand, BF16)]
    return pl.pallas_call(
        functools.partial(_nsa_attn_kernel, window=window),
        grid=(batch, NSA_KV_HEADS, nq, steps),
        in_specs=in_specs,
        out_specs=pl.BlockSpec((NSA_QT, width), lambda b, g, i, t: (b * nq + i, g)),
        out_shape=jax.ShapeDtypeStruct((batch * seq, NSA_HEADS * HEAD_DIM), F32),
        scratch_shapes=[pltpu.VMEM((NSA_GROUP, NSA_QT, LANES), F32),
                        pltpu.VMEM((NSA_GROUP, NSA_QT, LANES), F32),
                        pltpu.VMEM((NSA_GROUP, NSA_QT, width), F32)],
        compiler_params=_cparams(("parallel", "parallel", "parallel", "arbitrary")),
        name="nsa_window_prompt" if window else "nsa_selected_prompt",
    )(*args)


def head_rmsnorm(h, g, dtype):
    y = h * lax.rsqrt(jnp.mean(h * h, axis=-1, keepdims=True) + EPS)
    return (y * g.astype(jnp.float32)).astype(dtype)


def masked_softmax(s, valid, axes):
    s = jnp.where(valid, s, NEG_BIG)
    m = jnp.max(s, axis=axes, keepdims=True)
    e = jnp.where(valid, jnp.exp(s - m), 0.0)
    return e / jnp.maximum(jnp.sum(e, axis=axes, keepdims=True), TINY)


def split_cols(h, sizes):
    offs = [int(o) for o in np.cumsum(sizes)[:-1]]
    return jnp.split(h, offs, axis=-1)


def _repeat_group(a):
    B, T, G, dh = a.shape
    return jnp.tile(a.astype(BF16)[:, :, :, None, :], (1, 1, 1, NSA_GROUP, 1)).reshape(B * T, G * NSA_GROUP * dh)


def nsa_compress(rows, pos_emb, w):
    B, Tk, G, dh = rows.shape
    blk = rows.reshape(B, Tk // CMP_BLOCK, CMP_BLOCK, G, dh) + pos_emb[None, None, :, None, :]
    blk = blk.transpose(0, 1, 3, 2, 4).reshape(B, Tk // CMP_BLOCK, G, CMP_BLOCK * dh)
    return blk @ w


def nsa_compressed_selected(q, rows, q_pos, cmp_pos, cmp_w):
    B, Tq, H, dh = q.shape
    G, R = NSA_KV_HEADS, NSA_GROUP
    pad = (-rows.shape[1]) % SEL_BLOCK
    rows = jnp.pad(rows, ((0, 0), (0, pad), (0, 0), (0, 0), (0, 0)))
    Tp = rows.shape[1]
    scale = dh ** -0.5
    qg = q.reshape(B, Tq, G, R, dh)
    k_cmp = nsa_compress(rows[:, :, 0], cmp_pos[0], cmp_w[0])
    v_cmp = nsa_compress(rows[:, :, 1], cmp_pos[1], cmp_w[1])
    n_cmp = Tp // CMP_BLOCK
    s = jnp.einsum('btgrd,bngd->bgrtn', qg, k_cmp).astype(jnp.float32) * scale
    cmp_end = (jnp.arange(n_cmp) + 1) * CMP_BLOCK - 1
    p = masked_softmax(s, cmp_end[None, :] <= q_pos[:, None], -1)
    o_cmp = jnp.einsum('bgrtn,bngd->btgrd', p.astype(v_cmp.dtype), v_cmp).reshape(B, Tq, H, dh)
    n_sel = Tp // SEL_BLOCK
    imp = p.sum(axis=2).reshape(B, G, Tq, n_sel, SEL_BLOCK // CMP_BLOCK).sum(axis=-1)
    blk = jnp.arange(n_sel)
    forced = (blk[None, :] == 0) | (blk[None, :] == (q_pos // SEL_BLOCK)[:, None])
    avail = blk[None, :] * SEL_BLOCK <= q_pos[:, None]
    imp = jnp.where(avail, jnp.where(forced, FORCED_SCORE, imp), -1.0)
    _, idx = lax.top_k(imp, min(TOP_N, n_sel))
    n_top = idx.shape[-1]
    if Tq == Tp and Tq % NSA_KT == 0:
        sel = jnp.sum(jax.nn.one_hot(idx, n_sel, dtype=BF16), axis=-2).reshape(B * G * Tq, n_sel)
        o_sel = nsa_prompt_attention(q.astype(BF16).reshape(B * Tq, H * dh), _repeat_group(rows[:, :, 2]),
                                     _repeat_group(rows[:, :, 3]), sel, B, Tq)
        return o_cmp, o_sel.reshape(B, Tq, H, dh)
    if Tq <= SEL_QUERY_BLOCK:
        chosen = jnp.sum(jax.nn.one_hot(idx, n_sel, dtype=jnp.float32), axis=-2) > 0.5
        ok = jnp.repeat(chosen, SEL_BLOCK, axis=-1) & (jnp.arange(Tp)[None, :] <= q_pos[:, None])
        sc = jnp.einsum('bqgrd,bkgd->bgrqk', qg, rows[:, :, 2]).astype(jnp.float32) * scale
        w = masked_softmax(sc, ok[:, :, None], -1)
        o_sel = jnp.einsum('bgrqk,bkgd->bqgrd', w.astype(rows.dtype), rows[:, :, 3])
        return o_cmp, o_sel.reshape(B, Tq, H, dh)
    ks = rows[:, :, 2].reshape(B, n_sel, SEL_BLOCK, G, dh).transpose(0, 3, 1, 2, 4)
    vs = rows[:, :, 3].reshape(B, n_sel, SEL_BLOCK, G, dh).transpose(0, 3, 1, 2, 4)
    qb = SEL_QUERY_BLOCK if Tq % SEL_QUERY_BLOCK == 0 else Tq
    nqb = Tq // qb
    q_blocks = qg.reshape(B, nqb, qb, G, R, dh).swapaxes(0, 1)
    idx_blocks = idx.reshape(B, G, nqb, qb, n_top).transpose(2, 0, 1, 3, 4)
    pos_blocks = q_pos.reshape(nqb, qb)
    b_ix = jnp.arange(B)[:, None, None, None]
    g_ix = jnp.arange(G)[None, :, None, None]

    def sel_block(args):
        qc, ic, pc = args
        kg = ks[b_ix, g_ix, ic]
        vg = vs[b_ix, g_ix, ic]
        sc = jnp.einsum('bqgrd,bgqnkd->bgrqnk', qc, kg).astype(jnp.float32) * scale
        kpos = ic[..., None] * SEL_BLOCK + jnp.arange(SEL_BLOCK)
        ok = (kpos <= pc[None, None, :, None, None])[:, :, None]
        w = masked_softmax(sc, ok, (-2, -1))
        return jnp.einsum('bgrqnk,bgqnkd->bqgrd', w.astype(vg.dtype), vg)

    o_sel = lax.map(sel_block, (q_blocks, idx_blocks, pos_blocks))
    o_sel = o_sel.swapaxes(0, 1).reshape(B, Tq, H, dh)
    return o_cmp, o_sel


def window_core(qg, k, v, q_pos, k_pos):
    s = jnp.einsum('btgrd,bkgd->bgrtk', qg, k).astype(jnp.float32) * (qg.shape[-1] ** -0.5)
    dist = q_pos[:, None] - k_pos[None, :]
    ok = (dist >= 0) & (dist < WINDOW) & (k_pos[None, :] >= 0)
    w = masked_softmax(s, ok, -1)
    return jnp.einsum('bgrtk,bkgd->btgrd', w.astype(v.dtype), v)


def window_banded(qg, k, v):
    B, T = qg.shape[:2]
    qb = QUERY_BLOCK if T % QUERY_BLOCK == 0 else T
    nqb = T // qb
    span = WINDOW + qb
    kidx = jnp.arange(nqb)[:, None] * qb + jnp.arange(span)[None, :]
    pad = ((0, 0), (WINDOW, 0), (0, 0), (0, 0))
    kb = jnp.pad(k, pad)[:, kidx]
    vb = jnp.pad(v, pad)[:, kidx]
    q_pos = jnp.arange(T).reshape(nqb, qb)
    k_pos = kidx - WINDOW
    q_blocks = qg.reshape(B, nqb, qb, *qg.shape[2:])
    o = jax.vmap(window_core, in_axes=(1, 1, 1, 0, 0), out_axes=1)(q_blocks, kb, vb, q_pos, k_pos)
    return o.reshape(qg.shape)


def mlstm_chunkwise(q, k, v, i_pre, f_pre, C0, n0, m0):
    B, T, H, d = q.shape
    L = MLSTM_CHUNK if T % MLSTM_CHUNK == 0 else T
    nc = T // L
    f32 = jnp.float32
    q, k, v = q.astype(f32), k.astype(f32) * (d ** -0.5), v.astype(f32)
    log_f = jax.nn.log_sigmoid(f_pre.astype(f32))
    i_pre = i_pre.astype(f32)
    chunks = lambda a: a.reshape(B, nc, L, *a.shape[2:]).swapaxes(0, 1)
    causal = jnp.tril(jnp.ones((L, L), dtype=bool))

    def step(carry, xs):
        C, n, m = carry
        qc, kc, vc, ic, fc = xs
        b = jnp.cumsum(fc, axis=1).swapaxes(1, 2)
        it = ic.swapaxes(1, 2)
        log_d = jnp.where(causal, b[..., :, None] - b[..., None, :] + it[..., None, :], -jnp.inf)
        m_inter = b + m[..., None]
        m_t = jnp.maximum(m_inter, jnp.max(log_d, axis=-1))
        w = jnp.einsum('blhd,bshd->bhls', qc, kc) * jnp.exp(log_d - m_t[..., None])
        carry_w = jnp.exp(m_inter - m_t)
        num = jnp.einsum('bhls,bshd->blhd', w, vc) + jnp.einsum('blhd,bhde->blhe', qc, C) * carry_w.swapaxes(1, 2)[..., None]
        den = jnp.sum(w, axis=-1) + jnp.einsum('blhd,bhd->bhl', qc, n) * carry_w
        den = jnp.maximum(jnp.abs(den), jnp.exp(-m_t))
        h = num / den.swapaxes(1, 2)[..., None]
        m_new = m_t[..., -1]
        w_end = jnp.exp(b[..., -1:] - b + it - m_new[..., None])
        decay = jnp.exp(b[..., -1] + m - m_new)
        C_new = decay[..., None, None] * C + jnp.einsum('bhs,bshd,bshe->bhde', w_end, kc, vc)
        n_new = decay[..., None] * n + jnp.einsum('bhs,bshd->bhd', w_end, kc)
        return (C_new, n_new, m_new), h

    (C, n, m), hs = lax.scan(step, (C0.astype(f32), n0.astype(f32), m0.astype(f32)),
                             tuple(chunks(a) for a in (q, k, v, i_pre, log_f)))
    return hs.swapaxes(0, 1).reshape(B, T, H, d), C, n, m


def nsa_mlstm_core(proj, start, past_rows, win_buf, C0, n0, m0, nsa_cmp_pos, nsa_cmp_w,
                   mlstm_b_i, mlstm_b_f, mlstm_norm):
    B, T, _ = proj.shape
    (q_a, k_c, v_c, k_s, v_s, k_w, v_w, g_a, q_m, k_m, v_m, o_m, i_m, f_m) = split_cols(proj, IN0_SIZES)
    kvh = lambda a: a.reshape(B, T, NSA_KV_HEADS, HEAD_DIM)
    mh = lambda a: a.reshape(B, T, MLSTM_HEADS, MLSTM_DIM)
    q_a = q_a.reshape(B, T, NSA_HEADS, HEAD_DIM)
    q_pos = start + jnp.arange(T)
    new_rows = jnp.stack([kvh(k_c), kvh(v_c), kvh(k_s), kvh(v_s)], axis=2)
    rows_full = jnp.concatenate([past_rows.astype(new_rows.dtype), new_rows], axis=1)
    o_cmp, o_sel = nsa_compressed_selected(q_a, rows_full, q_pos, nsa_cmp_pos, nsa_cmp_w)
    win_rows = jnp.stack([kvh(k_w), kvh(v_w)], axis=2)
    qg = q_a.reshape(B, T, NSA_KV_HEADS, NSA_GROUP, HEAD_DIM)
    if win_buf is None:
        if T % NSA_KT == 0:
            o_win = nsa_prompt_attention(q_a.astype(BF16).reshape(B * T, NSA_HEADS * HEAD_DIM),
                                         _repeat_group(win_rows[:, :, 0]), _repeat_group(win_rows[:, :, 1]), None, B, T)
        else:
            o_win = window_banded(qg, win_rows[:, :, 0], win_rows[:, :, 1])
        new_win = win_rows[:, T - min(WINDOW, T):]
    else:
        n_buf = win_buf.shape[1]
        buf = jnp.concatenate([win_buf.astype(win_rows.dtype), win_rows], axis=1)
        k_pos = start - n_buf + jnp.arange(n_buf + T)
        o_win = window_core(qg, buf[:, :, 0], buf[:, :, 1], q_pos, k_pos)
        new_win = buf[:, T:]
    o_win = o_win.reshape(B, T, NSA_HEADS, HEAD_DIM)
    gates = jax.nn.sigmoid(g_a.reshape(B, T, NSA_HEADS, 3))
    o_a = gates[..., 0:1] * o_cmp + gates[..., 1:2] * o_sel + gates[..., 2:3] * o_win
    h_m, C, n, m = mlstm_chunkwise(mh(q_m), mh(k_m), mh(v_m), i_m + mlstm_b_i, f_m + mlstm_b_f, C0, n0, m0)
    h_m = head_rmsnorm(h_m, mlstm_norm, proj.dtype) * jax.nn.sigmoid(mh(o_m))
    mixed = jnp.concatenate([o_a.reshape(B, T, -1), h_m.reshape(B, T, -1)], axis=-1)
    return mixed, new_rows, new_win, C, n, m


def kernel(x_prompt, x_sample, cache_nsa_kv, state_nsa_win, state_mlstm_C, state_mlstm_n, state_mlstm_m, cache_sb_kv, page_table, norm_mix0, w_in0, nsa_cmp_pos, nsa_cmp_w, mlstm_b_i, mlstm_b_f, mlstm_norm, w_out0, norm_ffn0, w_gate0, w_up0, w_down0, norm_mix1, w_qkv1, w_out1, norm_ffn1, w_router1, w_gate1, w_up1, w_down1, norm_final):
    B, T, D = x_prompt.shape
    DB, TS, _ = x_sample.shape
    n_p = B * T
    past_len = page_table.shape[1] * cache_nsa_kv.shape[1]
    x = jnp.concatenate([x_prompt.reshape(n_p, D), x_sample.reshape(DB * TS, D)], axis=0)

    hn = rmsnorm_tokens(x, norm_mix0, BF16)
    in0_pad = (-IN0_COLS) % (9 * LANES)
    w_in = jnp.pad(w_in0, ((0, 0), (0, in0_pad))).astype(BF16)
    proj = matmul(hn, w_in, tn=(IN0_COLS + in0_pad) // 3)[:, :IN0_COLS]
    proj_p = proj[:n_p].reshape(B, T, IN0_COLS)
    proj_s = proj[n_p:].reshape(DB, TS, IN0_COLS)
    empty = jnp.zeros((B, 0, 4, NSA_KV_HEADS, HEAD_DIM), F32)
    c0 = jnp.zeros((B, MLSTM_HEADS, MLSTM_DIM, MLSTM_DIM), F32)
    n0 = jnp.zeros((B, MLSTM_HEADS, MLSTM_DIM), F32)
    m0 = jnp.zeros((B, MLSTM_HEADS), F32)
    mixed_p, nsa_rows_p, nsa_win_p, mC_p, mn_p, mm_p = nsa_mlstm_core(
        proj_p, 0, empty, None, c0, n0, m0, nsa_cmp_pos, nsa_cmp_w, mlstm_b_i, mlstm_b_f, mlstm_norm)
    past_rows = cache_nsa_kv[page_table].reshape(DB, past_len, 4, NSA_KV_HEADS, HEAD_DIM)
    mixed_s, nsa_rows_s, nsa_win_s, mC_s, mn_s, mm_s = nsa_mlstm_core(
        proj_s, past_len, past_rows, state_nsa_win, state_mlstm_C, state_mlstm_n, state_mlstm_m,
        nsa_cmp_pos, nsa_cmp_w, mlstm_b_i, mlstm_b_f, mlstm_norm)
    mixed = jnp.concatenate([mixed_p.reshape(n_p, D), mixed_s.reshape(DB * TS, D)], axis=0)
    x = matmul(mixed.astype(BF16), w_out0.astype(BF16), res=x)
    x = dense_ffn(rmsnorm_tokens(x, norm_ffn0, BF16), x, w_gate0, w_up0, w_down0)

    hn = rmsnorm_tokens(x, norm_mix1, BF16)
    qkv, qkv_b = matmul(hn, w_qkv1.astype(BF16), also_bf16=True)
    sb_rows_p = qkv[:n_p].reshape(B, T, 3, SB_HEADS, HEAD_DIM)[:, :, 1:]
    qkv_s = qkv[n_p:].reshape(DB, TS, 3, SB_HEADS, HEAD_DIM)
    o_p = sb_prompt_attention(qkv_b, B, T)
    o_s = sb_sample_attention(qkv_s, cache_sb_kv, page_table)
    sb_rows_s = qkv_s[:, :, 1:]
    o = jnp.concatenate([o_p, o_s.astype(BF16)], axis=0)
    x = matmul(o, w_out1.astype(BF16), res=x)
    hn, logits = rmsnorm_router(x, norm_ffn1, w_router1)
    x = moe_ffn(hn, logits[:, :N_EXPERTS], x, w_gate1, w_up1, w_down1)

    y = rmsnorm_tokens(x, norm_final, F32)
    y_prompt = y[:n_p].reshape(B, T, D)
    y_sample = y[n_p:].reshape(DB, TS, D)
    return (y_prompt, y_sample, nsa_rows_p, nsa_rows_s, nsa_win_p, nsa_win_s, mC_p, mn_p, mm_p,
            mC_s, mn_s, mm_s, sb_rows_p, sb_rows_s)
```
